```python
import jax, jax.numpy as jnp
from jax import lax
import numpy as np

D_MODEL = 1024
BATCH = 1
SEQ = 16384
DEPTH = 2

HEAD_DIM = 64
N_HEADS = D_MODEL // HEAD_DIM
N_SB_HEADS = N_HEADS // 2
N_MOBA_HEADS = N_HEADS - N_SB_HEADS
N_FOX_HEADS = N_HEADS
Q_BLOCK = 128
MOBA_BLOCK = 256
MOBA_TOPK = 3
ROPE_THETA = 10000.0
MEM_LEN = 256
XA_HEADS = 4
XA_HEAD_DIM = 64
XA_DIM = XA_HEADS * XA_HEAD_DIM
D_FF = 2816
CONV_WIDTH = 3
RMS_EPS = 1e-6
NEG_INF = -1e9
N_EVEN = (DEPTH + 1) // 2
N_ODD = DEPTH // 2

kernel_name = "hybrid_stickbreak_moba_fox_convffn"


def rmsnorm(x, g):
    xf = x.astype(jnp.float32)
    y = xf * lax.rsqrt(jnp.mean(xf * xf, axis=-1, keepdims=True) + RMS_EPS)
    return (y * g.astype(jnp.float32)).astype(x.dtype)


def rope(x, positions):
    half = HEAD_DIM // 2
    inv_freq = ROPE_THETA ** (-jnp.arange(half, dtype=jnp.float32) / half)
    ang = positions.astype(jnp.float32)[:, None, :, None] * inv_freq
    cos, sin = jnp.cos(ang), jnp.sin(ang)
    xf = x.astype(jnp.float32)
    x1, x2 = xf[..., :half], xf[..., half:]
    return jnp.concatenate([x1 * cos - x2 * sin, x2 * cos + x1 * sin], axis=-1).astype(x.dtype)


def to_chunks(a):
    B, H, S = a.shape[:3]
    a = a.reshape((B, H, S // Q_BLOCK, Q_BLOCK) + a.shape[3:])
    return jnp.moveaxis(a, 2, 0)


def from_chunks(o):
    n, B, H, qb, dh = o.shape
    return jnp.moveaxis(o, 0, 2).reshape(B, H, n * qb, dh)


def stick_breaking_attention(q, k, v):
    S, dh = q.shape[2], q.shape[3]
    scale = dh ** -0.5
    key_pos = jnp.arange(S)

    def block(args):
        qb, start = args
        z = jnp.einsum('bhtd,bhsd->bhts', qb, k).astype(jnp.float32) * scale
        qpos = start + jnp.arange(Q_BLOCK)
        past = key_pos[None, :] < qpos[:, None]
        log_beta = jax.nn.log_sigmoid(z)
        log_1mb = jnp.where(past, log_beta - z, 0.0)
        tail = lax.cumsum(log_1mb, axis=3, reverse=True) - log_1mb
        a = jnp.where(past, jnp.exp(log_beta + tail), 0.0)
        return jnp.einsum('bhts,bhsd->bhtd', a.astype(v.dtype), v)

    starts = jnp.arange(S // Q_BLOCK) * Q_BLOCK
    return from_chunks(lax.map(block, (to_chunks(q), starts)))


def moba_attention(q, k, v):
    B, H, S, dh = q.shape
    scale = dh ** -0.5
    nb = -(-S // MOBA_BLOCK)
    pad = nb * MOBA_BLOCK - S
    kb = jnp.pad(k, ((0, 0), (0, 0), (0, pad), (0, 0))).reshape(B, H, nb, MOBA_BLOCK, dh)
    vb = jnp.pad(v, ((0, 0), (0, 0), (0, pad), (0, 0))).reshape(B, H, nb, MOBA_BLOCK, dh)
    k_mean = jnp.mean(kb.astype(jnp.float32), axis=3)
    n_sel = min(MOBA_TOPK, nb)
    bi = jnp.arange(B)[:, None, None, None]
    hi = jnp.arange(H)[None, :, None, None]
    blk_ids = jnp.arange(nb)

    def block(args):
        qb, start = args
        own = start // MOBA_BLOCK
        qpos = start + jnp.arange(Q_BLOCK)
        gate = jnp.einsum('bhtd,bhnd->bhtn', qb.astype(jnp.float32), k_mean)
        gate = jnp.where(blk_ids < own, gate, NEG_INF)
        _, idx = lax.top_k(gate, n_sel)
        valid = jnp.arange(n_sel) < own
        kg = kb[bi, hi, idx]
        vg = vb[bi, hi, idx]
        s_sel = jnp.einsum('bhtd,bhtnkd->bhtnk', qb, kg).astype(jnp.float32) * scale
        s_sel = jnp.where(valid[:, None], s_sel, NEG_INF).reshape(B, H, Q_BLOCK, n_sel * MOBA_BLOCK)
        k_own = lax.dynamic_index_in_dim(kb, own, axis=2, keepdims=False)
        v_own = lax.dynamic_index_in_dim(vb, own, axis=2, keepdims=False)
        s_own = jnp.einsum('bhtd,bhkd->bhtk', qb, k_own).astype(jnp.float32) * scale
        own_pos = own * MOBA_BLOCK + jnp.arange(MOBA_BLOCK)
        s_own = jnp.where(own_pos[None, :] <= qpos[:, None], s_own, NEG_INF)
        p = jax.nn.softmax(jnp.concatenate([s_sel, s_own], axis=-1), axis=-1).astype(v.dtype)
        p_sel = p[..., :n_sel * MOBA_BLOCK].reshape(B, H, Q_BLOCK, n_sel, MOBA_BLOCK)
        p_own = p[..., n_sel * MOBA_BLOCK:]
        return (jnp.einsum('bhtnk,bhtnkd->bhtd', p_sel, vg)
                + jnp.einsum('bhtk,bhkd->bhtd', p_own, v_own))

    starts = jnp.arange(S // Q_BLOCK) * Q_BLOCK
    return from_chunks(lax.map(block, (to_chunks(q), starts)))


def forgetting_attention(q, k, v, log_f):
    S, dh = q.shape[2], q.shape[3]
    scale = dh ** -0.5
    key_pos = jnp.arange(S)
    c = lax.cumsum(log_f, axis=2)

    def block(args):
        qb, cb, start = args
        qpos = start + jnp.arange(Q_BLOCK)
        s = (jnp.einsum('bhtd,bhsd->bhts', qb, k).astype(jnp.float32) * scale
             + cb[..., None] - c[:, :, None, :])
        s = jnp.where(key_pos[None, :] <= qpos[:, None], s, NEG_INF)
        p = jax.nn.softmax(s, axis=-1).astype(v.dtype)
        return jnp.einsum('bhts,bhsd->bhtd', p, v)

    starts = jnp.arange(S // Q_BLOCK) * Q_BLOCK
    return from_chunks(lax.map(block, (to_chunks(q), to_chunks(c), starts)))


def split_heads(a, n_heads):
    B, S, _ = a.shape
    return a.reshape(B, S, n_heads, HEAD_DIM).transpose(0, 2, 1, 3)


def merge_heads(o):
    B, H, S, dh = o.shape
    return o.transpose(0, 2, 1, 3).reshape(B, S, H * dh)


def sb_moba_mixer(h, positions, w_in, w_out):
    qkv = h @ w_in
    q = split_heads(qkv[..., :D_MODEL], N_HEADS)
    k = split_heads(qkv[..., D_MODEL:2 * D_MODEL], N_HEADS)
    v = split_heads(qkv[..., 2 * D_MODEL:], N_HEADS)
    o_sb = stick_breaking_attention(q[:, :N_SB_HEADS], k[:, :N_SB_HEADS], v[:, :N_SB_HEADS])
    o_moba = moba_attention(rope(q[:, N_SB_HEADS:], positions),
                            rope(k[:, N_SB_HEADS:], positions), v[:, N_SB_HEADS:])
    return merge_heads(jnp.concatenate([o_sb, o_moba], axis=1)) @ w_out


def fox_mixer(h, w_in, b_f, w_out):
    proj = h @ w_in
    q = split_heads(proj[..., :D_MODEL], N_FOX_HEADS)
    k = split_heads(proj[..., D_MODEL:2 * D_MODEL], N_FOX_HEADS)
    v = split_heads(proj[..., 2 * D_MODEL:3 * D_MODEL], N_FOX_HEADS)
    f_logit = (proj[..., 3 * D_MODEL:] + b_f).astype(jnp.float32)
    log_f = jax.nn.log_sigmoid(f_logit).transpose(0, 2, 1)
    return merge_heads(forgetting_attention(q, k, v, log_f)) @ w_out


def memory_cross_attention(h, mem_h, w_q, w_kv, w_out):
    B, S, _ = h.shape
    M = mem_h.shape[1]
    q = (h @ w_q).reshape(B, S, XA_HEADS, XA_HEAD_DIM)
    kv = (mem_h @ w_kv).reshape(B, M, 2, XA_HEADS, XA_HEAD_DIM)
    k, v = kv[:, :, 0], kv[:, :, 1]
    s = jnp.einsum('bshd,bmhd->bhsm', q, k).astype(jnp.float32) * (XA_HEAD_DIM ** -0.5)
    p = jax.nn.softmax(s, axis=-1).astype(h.dtype)
    o = jnp.einsum('bhsm,bmhd->bshd', p, v).reshape(B, S, XA_DIM)
    return o @ w_out


def conv_ffn(h, w_up, conv_w, conv_b, w_down):
    S = h.shape[1]
    u = h @ w_up
    up = jnp.pad(u, ((0, 0), (CONV_WIDTH - 1, 0), (0, 0)))
    c = conv_b
    for i in range(CONV_WIDTH):
        c = c + up[:, i:i + S] * conv_w[i]
    gate, val = c[..., :D_FF], c[..., D_FF:]
    return (jax.nn.silu(gate) * val) @ w_down


def setup_inputs(seed: int = 0) -> dict:
    key = jax.random.key(seed)
    ks = jax.random.split(key, 24)
    nrm = lambda k, shape, s: jax.random.normal(k, shape, jnp.float32) * s
    D = D_MODEL
    positions = jnp.broadcast_to(jnp.arange(SEQ, dtype=jnp.int32)[None, :], (BATCH, SEQ))
    fox_b_f = (jnp.broadcast_to(jnp.linspace(1.0, 4.0, N_FOX_HEADS, dtype=jnp.float32), (N_ODD, N_FOX_HEADS))
               + nrm(ks[9], (N_ODD, N_FOX_HEADS), 0.1))
    return {
        "x": nrm(ks[0], (BATCH, SEQ, D), 1.0),
        "mem": nrm(ks[1], (BATCH, MEM_LEN, D), 1.0),
        "positions": positions,
        "norm_mix_g": 1.0 + nrm(ks[2], (DEPTH, D), 0.02),
        "norm_xa_g": 1.0 + nrm(ks[3], (DEPTH, D), 0.02),
        "norm_mem_g": 1.0 + nrm(ks[4], (DEPTH, D), 0.02),
        "norm_ffn_g": 1.0 + nrm(ks[5], (DEPTH, D), 0.02),
        "ab_w_in": nrm(ks[6], (N_EVEN, D, 3 * D), D ** -0.5),
        "ab_w_out": nrm(ks[7], (N_EVEN, D, D), D ** -0.5),
        "fox_w_in": nrm(ks[8], (N_ODD, D, 3 * D + N_FOX_HEADS), D ** -0.5),
        "fox_b_f": fox_b_f,
        "fox_w_out": nrm(ks[10], (N_ODD, D, D), D ** -0.5),
        "xa_w_q": nrm(ks[11], (DEPTH, D, XA_DIM), D ** -0.5),
        "xa_w_kv": nrm(ks[12], (DEPTH, D, 2 * XA_DIM), D ** -0.5),
        "xa_w_out": nrm(ks[13], (DEPTH, XA_DIM, D), XA_DIM ** -0.5),
        "ffn_w_up": nrm(ks[14], (DEPTH, D, 2 * D_FF), D ** -0.5),
        "ffn_conv_w": nrm(ks[15], (DEPTH, CONV_WIDTH, 2 * D_FF), CONV_WIDTH ** -0.5),
        "ffn_conv_b": nrm(ks[16], (DEPTH, 2 * D_FF), 0.02),
        "ffn_w_down": nrm(ks[17], (DEPTH, D_FF, D), D_FF ** -0.5),
        "final_norm_g": 1.0 + nrm(ks[18], (D,), 0.02),
    }


def reference(x, mem, positions, norm_mix_g, norm_xa_g, norm_mem_g, norm_ffn_g,
              ab_w_in, ab_w_out, fox_w_in, fox_b_f, fox_w_out,
              xa_w_q, xa_w_kv, xa_w_out, ffn_w_up, ffn_conv_w, ffn_conv_b, ffn_w_down,
              final_norm_g):
    h = x
    for layer in range(DEPTH):
        n = layer // 2
        hn = rmsnorm(h, norm_mix_g[layer])
        if layer % 2 == 0:
            h = h + sb_moba_mixer(hn, positions, ab_w_in[n], ab_w_out[n])
        else:
            h = h + fox_mixer(hn, fox_w_in[n], fox_b_f[n], fox_w_out[n])
        h = h + memory_cross_attention(rmsnorm(h, norm_xa_g[layer]), rmsnorm(mem, norm_mem_g[layer]),
                                       xa_w_q[layer], xa_w_kv[layer], xa_w_out[layer])
        h = h + conv_ffn(rmsnorm(h, norm_ffn_g[layer]), ffn_w_up[layer], ffn_conv_w[layer],
                         ffn_conv_b[layer], ffn_w_down[layer])
    return rmsnorm(h, final_norm_g)
```

```python
import functools

import numpy as np
import jax
import jax.numpy as jnp
from jax import lax
from jax.experimental import pallas as pl
from jax.experimental.pallas import tpu as pltpu

F32 = jnp.float32
BF16 = jnp.bfloat16

HEAD_DIM = 64
LANES = 128
MOBA_BLOCK = 256
MOBA_TOPK = 3
ROPE_THETA = 10000.0
XA_HEADS = 4
CONV_WIDTH = 3
RMS_EPS = 1e-6
NEG_INF = -1e9
KNOCKOUT = -3.0e38
EXP_ZERO = -104.0
ATT_TILE = 256
ROW_TILE = 512
PROJ_CHUNK = 512
FF_CHUNK = 256
VMEM_LIMIT = 56 * 1024 * 1024

_NT = (((1,), (1,)), ((), ()))


def _dot(a, b):
    return jnp.dot(a, b, preferred_element_type=F32)


def _dot_nt(a, b):
    return lax.dot_general(a, b, _NT, preferred_element_type=F32)


def _split2(x):
    hi = x.astype(BF16)
    lo = (x - hi.astype(F32)).astype(BF16)
    return hi, lo


def _split3(x):
    hi = x.astype(BF16)
    r = x - hi.astype(F32)
    mid = r.astype(BF16)
    lo = (r - mid.astype(F32)).astype(BF16)
    return hi, mid, lo


def _rms(x, g):
    return x * lax.rsqrt(jnp.mean(x * x, axis=-1, keepdims=True) + RMS_EPS) * g


def _log_sigmoid(z):
    return jnp.minimum(z, 0.0) - jnp.log(1.0 + jnp.exp(-jnp.abs(z)))


def _resident(shape):
    return pl.BlockSpec(shape, lambda *_: (0,) * len(shape), pipeline_mode=pl.Buffered(1))


def _params(*sem):
    return pltpu.CompilerParams(dimension_semantics=sem, vmem_limit_bytes=VMEM_LIMIT)


def _proj_kernel(*refs, n_chunks, rope_chunks, with_gates):
    if with_gates:
        (x_ref, g_ref, w_ref, wf_ref, wft_ref, bfr_ref, bfc_ref,
         o_ref, ccol_ref, crow_ref, carry_col, carry_row) = refs
    elif rope_chunks:
        x_ref, g_ref, w_ref, pos_ref, invf_ref, o_ref = refs
    else:
        x_ref, g_ref, w_ref, o_ref = refs
    tm = x_ref.shape[0]
    hn32 = _rms(x_ref[...], g_ref[...])
    hn = hn32.astype(BF16)

    if rope_chunks:
        ang = pos_ref[...].astype(F32) * invf_ref[...]
        cos, sin = jnp.cos(ang), jnp.sin(ang)
        lane = lax.broadcasted_iota(jnp.int32, (tm, LANES), 1)
        first_half = (lane % HEAD_DIM) < (HEAD_DIM // 2)
        sin_signed = jnp.where(first_half, -sin, sin)

    for c in range(n_chunks):
        y = _dot(hn, w_ref[:, c * PROJ_CHUNK:(c + 1) * PROJ_CHUNK])
        if c in rope_chunks:
            parts = []
            for s in range(PROJ_CHUNK // LANES):
                ys = y[:, s * LANES:(s + 1) * LANES]
                rot = jnp.where(first_half, pltpu.roll(ys, LANES - HEAD_DIM // 2, 1),
                                pltpu.roll(ys, HEAD_DIM // 2, 1))
                parts.append(ys * cos + rot * sin_signed)
            y = jnp.concatenate(parts, axis=1)
        o_ref[:, c * PROJ_CHUNK:(c + 1) * PROJ_CHUNK] = y.astype(o_ref.dtype)

    if with_gates:
        @pl.when(pl.program_id(0) == 0)
        def _():
            carry_col[...] = jnp.zeros_like(carry_col)
            carry_row[...] = jnp.zeros_like(carry_row)

        h_hi, h_lo = _split2(hn32)
        w_hi, w_lo = _split2(wf_ref[...])
        wt_hi, wt_lo = _split2(wft_ref[...])
        f_col = _dot(h_hi, w_hi) + _dot(h_lo, w_hi) + _dot(h_hi, w_lo) + bfr_ref[...]
        f_row = _dot_nt(wt_hi, h_hi) + _dot_nt(wt_hi, h_lo) + _dot_nt(wt_lo, h_hi) + bfc_ref[...]
        lf_col = _log_sigmoid(f_col)
        lf_row = _log_sigmoid(f_row)
        r = lax.broadcasted_iota(jnp.int32, (tm, tm), 0)
        cidx = lax.broadcasted_iota(jnp.int32, (tm, tm), 1)
        incl = (cidx <= r).astype(BF16)
        incl_t = (r <= cidx).astype(BF16)
        c_col = carry_col[...]
        for piece in _split3(lf_col):
            c_col = c_col + _dot(incl, piece)
        c_row = carry_row[...]
        for piece in _split3(lf_row):
            c_row = c_row + _dot(piece, incl_t)
        ccol_ref[...] = c_col
        crow_ref[...] = c_row
        carry_col[...] = c_col[tm - 1:tm, :]
        carry_row[...] = c_row[:, tm - 1:tm]


def _project(x, g, w, *, out_dtype=BF16, rope=None, gates=None):
    S, D = x.shape
    N = w.shape[1]
    tm = min(ROW_TILE, S)
    assert S % tm == 0 and N % PROJ_CHUNK == 0
    n_chunks = N // PROJ_CHUNK
    row = lambda i: (i, 0)
    in_specs = [pl.BlockSpec((tm, D), row), _resident((1, D)), _resident((D, N))]
    args = [x, g.reshape(1, D), w.astype(BF16)]
    out_shape = [jax.ShapeDtypeStruct((S, N), out_dtype)]
    out_specs = [pl.BlockSpec((tm, N), row)]
    scratch = []
    rope_chunks = ()
    if rope is not None:
        positions, rope_chunks = rope
        half = HEAD_DIM // 2
        inv_freq = ROPE_THETA ** (-np.arange(half, dtype=np.float32) / half)
        invf = jnp.asarray(np.tile(inv_freq, LANES // half).reshape(1, LANES).astype(np.float32))
        in_specs += [pl.BlockSpec((tm, 1), row), _resident((1, LANES))]
        args += [positions.reshape(S, 1), invf]
    if gates is not None:
        w_f, b_f = gates
        H = w_f.shape[1]
        in_specs += [_resident((D, LANES)), _resident((H, D)), _resident((1, LANES)), _resident((H, 1))]
        args += [jnp.pad(w_f, ((0, 0), (0, LANES - H))), w_f.T,
                 jnp.pad(b_f, (0, LANES - H)).reshape(1, LANES), b_f.reshape(H, 1)]
        out_shape += [jax.ShapeDtypeStruct((S, LANES), F32), jax.ShapeDtypeStruct((H, S), F32)]
        out_specs += [pl.BlockSpec((tm, LANES), row), pl.BlockSpec((H, tm), lambda i: (0, i))]
        scratch = [pltpu.VMEM((1, LANES), F32), pltpu.VMEM((H, 1), F32)]
    outs = pl.pallas_call(
        functools.partial(_proj_kernel, n_chunks=n_chunks, rope_chunks=tuple(rope_chunks),
                          with_gates=gates is not None),
        grid=(S // tm,), in_specs=in_specs, out_specs=out_specs, out_shape=out_shape,
        scratch_shapes=scratch, compiler_params=_params("arbitrary"),
        name="proj_gates" if gates is not None else ("proj_rope" if rope is not None else "proj"),
    )(*args)
    return outs if gates is not None else outs[0]


def _head_lanes(shape, a):
    lane = lax.broadcasted_iota(jnp.int32, shape, len(shape) - 1)
    return (lane >= a * HEAD_DIM) & (lane < (a + 1) * HEAD_DIM)


def _sb_kernel(q_ref, k_ref, v_ref, o_ref):
    t = q_ref.shape[0]
    qi = pl.program_id(1)
    scale = HEAD_DIM ** -0.5
    r = lax.broadcasted_iota(jnp.int32, (t, t), 0)
    c = lax.broadcasted_iota(jnp.int32, (t, t), 1)
    strict = c < r
    later = (r > c).astype(BF16)
    q2 = q_ref[...]
    out = jnp.zeros((t, LANES), F32)
    for a in range(2):
        mine = _head_lanes((t, LANES), a)
        q = jnp.where(mine, q2, jnp.zeros_like(q2))

        def tile(kb, carry, diagonal):
            k = k_ref[pl.ds(pl.multiple_of(kb * t, t), t), :]
            v = v_ref[pl.ds(pl.multiple_of(kb * t, t), t), :]
            z = _dot_nt(q, k) * scale
            lb = _log_sigmoid(z)
            l1 = lb - z
            if diagonal:
                l1 = jnp.where(strict, l1, 0.0)
            hi, lo = _split2(l1)
            tail = _dot(hi, later) + _dot(lo, later) + carry
            w = jnp.exp(lb + tail)
            if diagonal:
                w = jnp.where(strict, w, 0.0)
            pv = _dot(w.astype(BF16), v)
            return pv, tail[:, 0:1] + l1[:, 0:1]

        acc, carry = tile(qi, jnp.zeros((t, 1), F32), True)

        def cond(st):
            kb, carry, _ = st
            return jnp.logical_and(kb >= 0, jnp.max(carry) > EXP_ZERO)

        def body(st):
            kb, carry, acc = st
            pv, carry = tile(kb, carry, False)
            return kb - 1, carry, acc + pv

        _, _, acc = lax.while_loop(cond, body, (qi - 1, carry, acc))
        out = jnp.where(mine, acc, out)
    o_ref[...] = out.astype(o_ref.dtype)


def _flash_update(s, v, m, l, acc):
    m_new = jnp.maximum(m, jnp.max(s, axis=1, keepdims=True))
    alpha = jnp.exp(m - m_new)
    p = jnp.exp(s - m_new)
    l = alpha * l + jnp.sum(p, axis=1, keepdims=True)
    acc = alpha * acc + _dot(p.astype(BF16), v)
    return m_new, l, acc


def _flash_first(s, v):
    m = jnp.max(s, axis=1, keepdims=True)
    p = jnp.exp(s - m)
    return m, jnp.sum(p, axis=1, keepdims=True), _dot(p.astype(BF16), v)


def _moba_kernel(q_ref, k_ref, v_ref, o_ref, km32, km_hi, km_mid, km_lo):
    t = q_ref.shape[0]
    nb = k_ref.shape[0] // MOBA_BLOCK
    qi = pl.program_id(1)
    scale = HEAD_DIM ** -0.5

    @pl.when(qi == 0)
    def _():
        def centroid(b, _):
            blk = k_ref[pl.ds(pl.multiple_of(b * MOBA_BLOCK, MOBA_BLOCK), MOBA_BLOCK), :].astype(F32)
            km32[pl.ds(b, 1), :] = jnp.sum(blk, axis=0, keepdims=True) * (1.0 / MOBA_BLOCK)
            return 0
        lax.fori_loop(0, nb, centroid, 0)
        hi, mid, lo = _split3(km32[...])
        km_hi[...] = hi
        km_mid[...] = mid
        km_lo[...] = lo

    r = lax.broadcasted_iota(jnp.int32, (t, t), 0)
    c = lax.broadcasted_iota(jnp.int32, (t, t), 1)
    causal = c <= r
    blk_id = lax.broadcasted_iota(jnp.int32, (t, nb), 1)
    blk_f = blk_id.astype(F32)
    q2 = q_ref[...]
    k_own = k_ref[pl.ds(pl.multiple_of(qi * t, t), t), :]
    v_own = v_ref[pl.ds(pl.multiple_of(qi * t, t), t), :]
    out = jnp.zeros((t, LANES), F32)
    for a in range(2):
        mine = _head_lanes((t, LANES), a)
        q = jnp.where(mine, q2, jnp.zeros_like(q2))

        gate = _dot_nt(q, km_hi[...]) + _dot_nt(q, km_mid[...]) + _dot_nt(q, km_lo[...])
        g = jnp.where(blk_id < qi, gate, NEG_INF)
        sel = jnp.zeros((t, nb), F32)
        for rank in range(MOBA_TOPK):
            best = jnp.max(g, axis=1, keepdims=True)
            idx = jnp.min(jnp.where(g == best, blk_f, float(nb)), axis=1, keepdims=True)
            pick = blk_f == idx
            sel = jnp.where(pick, jnp.where(rank < qi, 1.0, 0.0), sel)
            g = jnp.where(pick, KNOCKOUT, g)

        s = jnp.where(causal, _dot_nt(q, k_own) * scale, NEG_INF)
        m, l, acc = _flash_first(s, v_own)

        def body(j, st):
            m, l, acc = st
            k = k_ref[pl.ds(pl.multiple_of(j * t, t), t), :]
            v = v_ref[pl.ds(pl.multiple_of(j * t, t), t), :]
            chosen = jnp.sum(jnp.where(blk_id == j, sel, 0.0), axis=1, keepdims=True)
            s = jnp.where(chosen > 0.0, _dot_nt(q, k) * scale, NEG_INF)
            return _flash_update(s, v, m, l, acc)

        m, l, acc = lax.fori_loop(0, qi, body, (m, l, acc))
        out = jnp.where(mine, acc / l, out)
    o_ref[...] = out.astype(o_ref.dtype)


def _fox_kernel(q_ref, k_ref, v_ref, ccol_ref, crow_ref, o_ref):
    t = q_ref.shape[0]
    hp = pl.program_id(0)
    qi = pl.program_id(1)
    scale = HEAD_DIM ** -0.5
    r = lax.broadcasted_iota(jnp.int32, (t, t), 0)
    c = lax.broadcasted_iota(jnp.int32, (t, t), 1)
    causal = c <= r
    lane = lax.broadcasted_iota(jnp.int32, (t, LANES), 1)
    q2 = q_ref[...]
    ccol = ccol_ref[...]
    k_own = k_ref[pl.ds(pl.multiple_of(qi * t, t), t), :]
    v_own = v_ref[pl.ds(pl.multiple_of(qi * t, t), t), :]
    out = jnp.zeros((t, LANES), F32)
    for a in range(2):
        h = 2 * hp + a
        mine = _head_lanes((t, LANES), a)
        q = jnp.where(mine, q2, jnp.zeros_like(q2))
        ct = jnp.sum(jnp.where(lane == h, ccol, 0.0), axis=1, keepdims=True)

        def logits(k, j):
            cs = crow_ref[pl.ds(h, 1), pl.ds(pl.multiple_of(j * t, t), t)]
            return _dot_nt(q, k) * scale + ct - cs

        s = jnp.where(causal, logits(k_own, qi), NEG_INF)
        m, l, acc = _flash_first(s, v_own)

        def body(j, st):
            m, l, acc = st
            k = k_ref[pl.ds(pl.multiple_of(j * t, t), t), :]
            v = v_ref[pl.ds(pl.multiple_of(j * t, t), t), :]
            return _flash_update(logits(k, j), v, m, l, acc)

        m, l, acc = lax.fori_loop(0, qi, body, (m, l, acc))
        out = jnp.where(mine, acc / l, out)
    o_ref[...] = out.astype(o_ref.dtype)


def _attention(kind, qkv, out_cols, pair0, n_pairs, gates=None):
    S = qkv.shape[0]
    t = min(ATT_TILE, S)
    assert S % t == 0 and t == MOBA_BLOCK
    d_blocks = out_cols // LANES
    q_spec = pl.BlockSpec((t, LANES), lambda p, i: (i, pair0 + p))
    k_spec = pl.BlockSpec((S, LANES), lambda p, i: (0, d_blocks + pair0 + p))
    v_spec = pl.BlockSpec((S, LANES), lambda p, i: (0, 2 * d_blocks + pair0 + p))
    o_spec = pl.BlockSpec((t, LANES), lambda p, i: (i, p))
    in_specs = [q_spec, k_spec, v_spec]
    args = [qkv, qkv, qkv]
    scratch = []
    if kind == "sb":
        body = _sb_kernel
    elif kind == "moba":
        body = _moba_kernel
        nb = S // MOBA_BLOCK
        scratch = [pltpu.VMEM((nb, LANES), F32)] + [pltpu.VMEM((nb, LANES), BF16)] * 3
    else:
        body = _fox_kernel
        c_col, c_row = gates
        in_specs += [pl.BlockSpec((t, LANES), lambda p, i: (i, 0)),
                     pl.BlockSpec(c_row.shape, lambda p, i: (0, 0))]
        args += [c_col, c_row]
    return pl.pallas_call(
        body, grid=(n_pairs, S // t), in_specs=in_specs, out_specs=o_spec,
        out_shape=jax.ShapeDtypeStruct((S, n_pairs * LANES), BF16),
        scratch_shapes=scratch, compiler_params=_params("arbitrary", "arbitrary"), name=kind,
    )(*args)


def _memkv_kernel(mem_ref, g_ref, w_ref, o_ref):
    o_ref[...] = _dot(_rms(mem_ref[...], g_ref[...]).astype(BF16), w_ref[...]).astype(o_ref.dtype)


def _memory_kv(mem, g, w_kv):
    M, D = mem.shape
    N = w_kv.shape[1]
    return pl.pallas_call(
        _memkv_kernel, out_shape=jax.ShapeDtypeStruct((M, N), BF16), name="memkv",
        compiler_params=pltpu.CompilerParams(vmem_limit_bytes=VMEM_LIMIT),
    )(mem, g.reshape(1, D), w_kv.astype(BF16))


def _post_kernel(*refs, n_att):
    att_refs = refs[:n_att]
    h_ref, wo_ref, g_ref, wq_ref, kv_ref, wxo_ref, o_ref = refs[n_att:]
    tm = h_ref.shape[0]
    xa_dim = wq_ref.shape[1]
    hd = xa_dim // XA_HEADS
    h1 = h_ref[...]
    col = 0
    for a_ref in att_refs:
        w = a_ref.shape[1]
        h1 = h1 + _dot(a_ref[...], wo_ref[col:col + w, :])
        col += w
    q = _dot(_rms(h1, g_ref[...]).astype(BF16), wq_ref[...]).astype(BF16)
    kv = kv_ref[...]
    k = kv[:, :xa_dim]
    v = kv[:, xa_dim:]
    lane = lax.broadcasted_iota(jnp.int32, (tm, xa_dim), 1)
    o = jnp.zeros((tm, xa_dim), F32)
    for hh in range(XA_HEADS):
        mine = (lane >= hh * hd) & (lane < (hh + 1) * hd)
        s = _dot_nt(jnp.where(mine, q, jnp.zeros_like(q)), k) * (hd ** -0.5)
        m = jnp.max(s, axis=1, keepdims=True)
        p = jnp.exp(s - m)
        l = jnp.sum(p, axis=1, keepdims=True)
        o = jnp.where(mine, _dot(p.astype(BF16), v) / l, o)
    o_ref[...] = h1 + _dot(o.astype(BF16), wxo_ref[...])


def _post(att_parts, h, w_out, g_xa, w_q, kv, w_xo):
    S, D = h.shape
    tm = min(ROW_TILE, S)
    row = lambda i: (i, 0)
    in_specs = [pl.BlockSpec((tm, a.shape[1]), row) for a in att_parts]
    in_specs += [pl.BlockSpec((tm, D), row), _resident(w_out.shape), _resident((1, D)),
                 _resident(w_q.shape), _resident(kv.shape), _resident(w_xo.shape)]
    return pl.pallas_call(
        functools.partial(_post_kernel, n_att=len(att_parts)),
        grid=(S // tm,), in_specs=in_specs, out_specs=pl.BlockSpec((tm, D), row),
        out_shape=jax.ShapeDtypeStruct((S, D), F32), compiler_params=_params("arbitrary"), name="post",
    )(*att_parts, h, w_out.astype(BF16), g_xa.reshape(1, D), w_q.astype(BF16), kv, w_xo.astype(BF16))


def _ffn_kernel(*refs, d_ff, final):
    if final:
        h_ref, g_ref, wup_ref, cw_ref, cb_ref, wdn_ref, gfin_ref, o_ref, prev_ref = refs
    else:
        h_ref, g_ref, wup_ref, cw_ref, cb_ref, wdn_ref, o_ref, prev_ref = refs
    tm = h_ref.shape[0]

    @pl.when(pl.program_id(0) == 0)
    def _():
        prev_ref[...] = jnp.zeros_like(prev_ref)

    h = h_ref[...]
    hn = _rms(h, g_ref[...]).astype(BF16)
    rowi = lax.broadcasted_iota(jnp.int32, (tm, FF_CHUNK), 0)

    def conv(col0):
        cols = slice(col0, col0 + FF_CHUNK)
        u = _dot(hn, wup_ref[:, cols])
        prev = prev_ref[:, cols]
        p1, p2 = prev[7:8, :], prev[6:7, :]
        u1 = jnp.where(rowi == 0, p1, pltpu.roll(u, 1, 0))
        u2 = jnp.where(rowi == 0, p2, jnp.where(rowi == 1, p1, pltpu.roll(u, 2, 0)))
        prev_ref[:, cols] = u[tm - 8:tm, :]
        w = cw_ref[:, cols]
        return cb_ref[:, cols] + u2 * w[0:1, :] + u1 * w[1:2, :] + u * w[2:3, :]

    acc = h
    for c in range(d_ff // FF_CHUNK):
        gate = conv(c * FF_CHUNK)
        val = conv(d_ff + c * FF_CHUNK)
        act = gate * (1.0 / (1.0 + jnp.exp(-gate))) * val
        acc = acc + _dot(act.astype(BF16), wdn_ref[c * FF_CHUNK:(c + 1) * FF_CHUNK, :])
    if final:
        acc = _rms(acc, gfin_ref[...])
    o_ref[...] = acc


def _ffn(h, g, w_up, conv_w, conv_b, w_down, g_final=None):
    S, D = h.shape
    d_ff = w_down.shape[0]
    tm = min(ROW_TILE, S)
    assert d_ff % FF_CHUNK == 0 and conv_w.shape[0] == CONV_WIDTH
    row = lambda i: (i, 0)
    in_specs = [pl.BlockSpec((tm, D), row), _resident((1, D)), _resident(w_up.shape),
                _resident(conv_w.shape), _resident((1, 2 * d_ff)), _resident(w_down.shape)]
    args = [h, g.reshape(1, D), w_up.astype(BF16), conv_w, conv_b.reshape(1, 2 * d_ff), w_down.astype(BF16)]
    if g_final is not None:
        in_specs.append(_resident((1, D)))
        args.append(g_final.reshape(1, D))
    return pl.pallas_call(
        functools.partial(_ffn_kernel, d_ff=d_ff, final=g_final is not None),
        grid=(S // tm,), in_specs=in_specs, out_specs=pl.BlockSpec((tm, D), row),
        out_shape=jax.ShapeDtypeStruct((S, D), F32),
        scratch_shapes=[pltpu.VMEM((8, 2 * d_ff), F32)],
        compiler_params=_params("arbitrary"), name="ffn",
    )(*args)


def kernel(x, mem, positions, norm_mix_g, norm_xa_g, norm_mem_g, norm_ffn_g, ab_w_in, ab_w_out, fox_w_in, fox_b_f, fox_w_out, xa_w_q, xa_w_kv, xa_w_out, ffn_w_up, ffn_conv_w, ffn_conv_b, ffn_w_down, final_norm_g):
    B, S, D = x.shape
    depth = norm_mix_g.shape[0]
    n_pairs = D // LANES
    outs = []
    for b in range(B):
        h = x[b]
        for layer in range(depth):
            n = layer // 2
            if layer % 2 == 0:
                half_chunks = D // PROJ_CHUNK // 2
                rope_chunks = tuple(range(half_chunks, 2 * half_chunks)) + tuple(
                    range(3 * half_chunks, 4 * half_chunks))
                qkv = _project(h, norm_mix_g[layer], ab_w_in[n], rope=(positions[b], rope_chunks))
                att = [_attention("sb", qkv, D, 0, n_pairs // 2),
                       _attention("moba", qkv, D, n_pairs // 2, n_pairs // 2)]
                w_out = ab_w_out[n]
            else:
                w_in = fox_w_in[n]
                qkv, c_col, c_row = _project(h, norm_mix_g[layer], w_in[:, :3 * D],
                                             gates=(w_in[:, 3 * D:], fox_b_f[n]))
                att = [_attention("fox", qkv, D, 0, n_pairs, gates=(c_col, c_row))]
                w_out = fox_w_out[n]
            kv = _memory_kv(mem[b], norm_mem_g[layer], xa_w_kv[layer])
            h = _post(att, h, w_out, norm_xa_g[layer], xa_w_q[layer], kv, xa_w_out[layer])
            h = _ffn(h, norm_ffn_g[layer], ffn_w_up[layer], ffn_conv_w[layer], ffn_conv_b[layer],
                     ffn_w_down[layer], final_norm_g if layer == depth - 1 else None)
        outs.append(h)
    return jnp.stack(outs)
```

```python
import functools

import numpy as np
import jax
import jax.numpy as jnp
from jax import lax
from jax.experimental import pallas as pl
from jax.experimental.pallas import tpu as pltpu

F32 = jnp.float32
BF16 = jnp.bfloat16

HEAD_DIM = 64
LANES = 128
MOBA_BLOCK = 256
MOBA_TOPK = 3
ROPE_THETA = 10000.0
XA_HEADS = 4
CONV_WIDTH = 3
RMS_EPS = 1e-6
NEG_INF = -1e9
KNOCKOUT = -3.0e38
EXP_ZERO = -104.0
BOUND_SLACK = 1.0
ATT_TILE = 256
ROW_TILE = 512
PROJ_CHUNK = 512
FF_CHUNK = 256
VMEM_LIMIT = 56 * 1024 * 1024

_NT = (((1,), (1,)), ((), ()))


def _dot(a, b):
    return jnp.dot(a, b, preferred_element_type=F32)


def _dot_nt(a, b):
    return lax.dot_general(a, b, _NT, preferred_element_type=F32)


def _split2(x):
    hi = x.astype(BF16)
    lo = (x - hi.astype(F32)).astype(BF16)
    return hi, lo


def _split3(x):
    hi = x.astype(BF16)
    r = x - hi.astype(F32)
    mid = r.astype(BF16)
    lo = (r - mid.astype(F32)).astype(BF16)
    return hi, mid, lo


def _rms(x, g):
    return x * lax.rsqrt(jnp.mean(x * x, axis=-1, keepdims=True) + RMS_EPS) * g


def _log_sigmoid(z):
    return jnp.minimum(z, 0.0) - jnp.log(1.0 + jnp.exp(-jnp.abs(z)))


def _resident(shape):
    return pl.BlockSpec(shape, lambda *_: (0,) * len(shape), pipeline_mode=pl.Buffered(1))


def _params(*sem):
    return pltpu.CompilerParams(dimension_semantics=sem, vmem_limit_bytes=VMEM_LIMIT)


def _proj_kernel(*refs, n_chunks, rope_chunks, with_gates):
    if with_gates:
        x_ref, g_ref, w_ref, wft_ref, bfc_ref, o_ref, crow_ref, carry_row = refs
    elif rope_chunks:
        x_ref, g_ref, w_ref, pos_ref, o_ref = refs
    else:
        x_ref, g_ref, w_ref, o_ref = refs
    tm = x_ref.shape[0]
    hn32 = _rms(x_ref[...], g_ref[...])
    hn = hn32.astype(BF16)

    if rope_chunks:
        half = HEAD_DIM // 2
        lane = lax.broadcasted_iota(jnp.int32, (tm, LANES), 1)
        first_half = (lane % HEAD_DIM) < half
        freq_id = (lax.broadcasted_iota(jnp.int32, (1, LANES), 1) % half).astype(F32)
        inv_freq = jnp.exp(freq_id * (-np.log(ROPE_THETA) / half))
        ang = pos_ref[...].astype(F32) * inv_freq
        cos, sin = jnp.cos(ang), jnp.sin(ang)
        sin_signed = jnp.where(first_half, -sin, sin)

    for c in range(n_chunks):
        y = _dot(hn, w_ref[:, c * PROJ_CHUNK:(c + 1) * PROJ_CHUNK])
        if c in rope_chunks:
            parts = []
            for s in range(PROJ_CHUNK // LANES):
                ys = y[:, s * LANES:(s + 1) * LANES]
                rot = jnp.where(first_half, pltpu.roll(ys, LANES - HEAD_DIM // 2, 1),
                                pltpu.roll(ys, HEAD_DIM // 2, 1))
                parts.append(ys * cos + rot * sin_signed)
            y = jnp.concatenate(parts, axis=1)
        o_ref[:, c * PROJ_CHUNK:(c + 1) * PROJ_CHUNK] = y.astype(o_ref.dtype)

    if with_gates:
        @pl.when(pl.program_id(0) == 0)
        def _():
            carry_row[...] = jnp.zeros_like(carry_row)

        h_hi, h_lo = _split2(hn32)
        wt_hi, wt_lo = _split2(wft_ref[...])
        f_row = _dot_nt(wt_hi, h_hi) + _dot_nt(wt_hi, h_lo) + _dot_nt(wt_lo, h_hi) + bfc_ref[...]
        lf_row = _log_sigmoid(f_row)
        r = lax.broadcasted_iota(jnp.int32, (tm, tm), 0)
        cidx = lax.broadcasted_iota(jnp.int32, (tm, tm), 1)
        incl_t = (r <= cidx).astype(BF16)
        c_row = carry_row[...]
        for piece in _split3(lf_row):
            c_row = c_row + _dot(piece, incl_t)
        crow_ref[...] = c_row
        carry_row[...] = c_row[:, tm - 1:tm]


def _project(x, g, w, *, out_dtype=BF16, rope=None, gates=None):
    S, D = x.shape
    N = w.shape[1]
    tm = min(ROW_TILE, S)
    assert S % tm == 0 and N % PROJ_CHUNK == 0
    n_chunks = N // PROJ_CHUNK
    row = lambda i: (i, 0)
    in_specs = [pl.BlockSpec((tm, D), row), _resident((1, D)), _resident((D, N))]
    args = [x, g.reshape(1, D), w.astype(BF16)]
    out_shape = [jax.ShapeDtypeStruct((S, N), out_dtype)]
    out_specs = [pl.BlockSpec((tm, N), row)]
    scratch = []
    rope_chunks = ()
    if rope is not None:
        positions, rope_chunks = rope
        in_specs += [pl.BlockSpec((tm, 1), row)]
        args += [positions.reshape(S, 1)]
    if gates is not None:
        w_f, b_f = gates
        H = w_f.shape[1]
        in_specs += [_resident((H, D)), _resident((H, 1))]
        args += [w_f.T, b_f.reshape(H, 1)]
        out_shape += [jax.ShapeDtypeStruct((H, S), F32)]
        out_specs += [pl.BlockSpec((H, tm), lambda i: (0, i))]
        scratch = [pltpu.VMEM((H, 1), F32)]
    outs = pl.pallas_call(
        functools.partial(_proj_kernel, n_chunks=n_chunks, rope_chunks=tuple(rope_chunks),
                          with_gates=gates is not None),
        grid=(S // tm,), in_specs=in_specs, out_specs=out_specs, out_shape=out_shape,
        scratch_shapes=scratch, compiler_params=_params("arbitrary"),
        name="proj_gates" if gates is not None else ("proj_rope" if rope is not None else "proj"),
    )(*args)
    return outs if gates is not None else outs[0]


def _head_lanes(shape, a):
    lane = lax.broadcasted_iota(jnp.int32, shape, len(shape) - 1)
    return (lane >= a * HEAD_DIM) & (lane < (a + 1) * HEAD_DIM)


def _pair_queries(q_ref, scale):
    q2 = q_ref[...] * jnp.asarray(scale, q_ref.dtype)
    return [jnp.where(_head_lanes(q2.shape, a), q2, jnp.zeros_like(q2)) for a in range(2)]


def _kv_block(k_ref, v_ref, j, t):
    rows = pl.ds(pl.multiple_of(j * t, t), t)
    return k_ref[rows, :], v_ref[rows, :]


def _sb_kernel(q_ref, k_ref, v_ref, o_ref):
    t = q_ref.shape[0]
    qi = pl.program_id(1)
    scale = HEAD_DIM ** -0.5
    r = lax.broadcasted_iota(jnp.int32, (t, t), 0)
    c = lax.broadcasted_iota(jnp.int32, (t, t), 1)
    strict = c < r
    later = (r > c).astype(BF16)
    qs = _pair_queries(q_ref, scale)

    def tile(q, k, v, carry, diagonal):
        z = _dot_nt(q, k)
        lb = _log_sigmoid(z)
        l1 = lb - z
        if diagonal:
            l1 = jnp.where(strict, l1, 0.0)
        hi, lo = _split2(l1)
        tail = _dot(hi, later) + _dot(lo, later) + carry
        w = jnp.exp(lb + tail)
        if diagonal:
            w = jnp.where(strict, w, 0.0)
        return _dot(w.astype(BF16), v), tail[:, 0:1] + l1[:, 0:1]

    def block(kb, carries, diagonal):
        k, v = _kv_block(k_ref, v_ref, kb, t)
        return [tile(qs[a], k, v, carries[a], diagonal) for a in range(2)]

    zero = jnp.zeros((t, 1), F32)
    (acc0, carry0), (acc1, carry1) = block(qi, (zero, zero), True)

    def cond(st):
        kb, carry0, carry1, _, _ = st
        return jnp.logical_and(kb >= 0, jnp.max(jnp.maximum(carry0, carry1)) > EXP_ZERO)

    def body(st):
        kb, carry0, carry1, acc0, acc1 = st
        (pv0, carry0), (pv1, carry1) = block(kb, (carry0, carry1), False)
        return kb - 1, carry0, carry1, acc0 + pv0, acc1 + pv1

    _, _, _, acc0, acc1 = lax.while_loop(cond, body, (qi - 1, carry0, carry1, acc0, acc1))
    o_ref[...] = jnp.where(_head_lanes((t, LANES), 0), acc0, acc1).astype(o_ref.dtype)


class _Flash:
    def __init__(self, qs, s, rm, m, l, acc):
        self.qs, self.s, self.rm, self.m, self.l, self.acc = qs, s, rm, m, l, acc

    def init(self, q_ref, scale):
        q2 = q_ref[...] * jnp.asarray(scale, q_ref.dtype)
        for a in range(2):
            self.qs[a] = jnp.where(_head_lanes(q2.shape, a), q2, jnp.zeros_like(q2))
            self.m[a] = jnp.full(self.m.shape[1:], KNOCKOUT, F32)
            self.l[a] = jnp.zeros(self.l.shape[1:], F32)
            self.acc[a] = jnp.zeros(self.acc.shape[1:], F32)

    def logits(self, buf, k, fix):
        for a in range(2):
            s = fix(a, _dot_nt(self.qs[a], k))
            self.s[2 * buf + a] = s
            self.rm[2 * buf + a] = jnp.broadcast_to(jnp.max(s, axis=1, keepdims=True), self.rm.shape[1:])

    def accumulate(self, buf, v):
        for a in range(2):
            m_new = jnp.maximum(self.m[a], self.rm[2 * buf + a])
            alpha = jnp.exp(self.m[a] - m_new)
            s = self.s[2 * buf + a]
            p = jnp.exp(s - jnp.tile(m_new, (1, s.shape[1] // LANES)))
            part = p[:, :LANES]
            for c in range(1, s.shape[1] // LANES):
                part = part + p[:, c * LANES:(c + 1) * LANES]
            self.l[a] = alpha * self.l[a] + part
            self.acc[a] = alpha * self.acc[a] + _dot(p.astype(BF16), v)
            self.m[a] = m_new

    def result(self):
        t = self.acc.shape[1]
        out = [self.acc[a] / jnp.sum(self.l[a], axis=1, keepdims=True) for a in range(2)]
        return jnp.where(_head_lanes((t, LANES), 0), out[0], out[1])


def _flash_scratch(t):
    return [pltpu.VMEM((2, t, LANES), BF16), pltpu.VMEM((4, t, t), F32), pltpu.VMEM((4, t, LANES), F32),
            pltpu.VMEM((2, t, LANES), F32), pltpu.VMEM((2, t, LANES), F32), pltpu.VMEM((2, t, LANES), F32)]


def _moba_kernel(q_ref, k_ref, v_ref, o_ref, km32, km_hi, km_mid, km_lo, sel_ref, *flash_refs):
    t = q_ref.shape[0]
    nb = k_ref.shape[0] // MOBA_BLOCK
    qi = pl.program_id(1)
    scale = HEAD_DIM ** -0.5

    @pl.when(qi == 0)
    def _():
        def centroid(b, _):
            blk = k_ref[pl.ds(pl.multiple_of(b * MOBA_BLOCK, MOBA_BLOCK), MOBA_BLOCK), :].astype(F32)
            km32[pl.ds(b, 1), :] = jnp.sum(blk, axis=0, keepdims=True) * (1.0 / MOBA_BLOCK)
            return 0
        lax.fori_loop(0, nb, centroid, 0)
        hi, mid, lo = _split3(km32[...])
        km_hi[...] = hi
        km_mid[...] = mid
        km_lo[...] = lo

    r = lax.broadcasted_iota(jnp.int32, (t, t), 0)
    c = lax.broadcasted_iota(jnp.int32, (t, t), 1)
    causal = c <= r
    blk_id = lax.broadcasted_iota(jnp.int32, (t, nb), 1)
    blk_f = blk_id.astype(F32)
    fl = _Flash(*flash_refs)
    fl.init(q_ref, scale)

    for a in range(2):
        q = fl.qs[a]
        gate = _dot_nt(q, km_hi[...]) + _dot_nt(q, km_mid[...]) + _dot_nt(q, km_lo[...])
        g = jnp.where(blk_id < qi, gate, NEG_INF)
        sel = jnp.zeros((t, nb), F32)
        for rank in range(MOBA_TOPK):
            best = jnp.max(g, axis=1, keepdims=True)
            idx = jnp.min(jnp.where(g == best, blk_f, float(nb)), axis=1, keepdims=True)
            pick = blk_f == idx
            sel = jnp.where(pick, jnp.where(rank < qi, 1.0, 0.0), sel)
            g = jnp.where(pick, KNOCKOUT, g)
        sel_ref[a] = sel

    def past_logits(buf, j):
        k = k_ref[pl.ds(pl.multiple_of(jnp.minimum(j, nb - 1) * t, t), t), :]
        j_sel = jnp.where(j < qi, j, -1)

        def fix(a, z):
            chosen = jnp.sum(jnp.where(blk_id == j_sel, sel_ref[a], 0.0), axis=1, keepdims=True)
            return jnp.where(chosen > 0.0, z, NEG_INF)
        fl.logits(buf, k, fix)

    def v_block(j):
        return v_ref[pl.ds(pl.multiple_of(jnp.minimum(j, nb - 1) * t, t), t), :]

    k_own, v_own = _kv_block(k_ref, v_ref, qi, t)
    fl.logits(0, k_own, lambda a, z: jnp.where(causal, z, NEG_INF))
    past_logits(1, 0)
    fl.accumulate(0, v_own)

    def body(i, _):
        past_logits(0, 2 * i + 1)
        fl.accumulate(1, v_block(2 * i))
        past_logits(1, 2 * i + 2)
        fl.accumulate(0, v_block(2 * i + 1))
        return 0

    lax.fori_loop(0, (qi + 1) // 2, body, 0)
    o_ref[...] = fl.result().astype(o_ref.dtype)


def _fox_kernel(cend_ref, q_ref, k_ref, v_ref, crow_ref, o_ref, knorm_ref, *flash_refs):
    t = q_ref.shape[0]
    nb = k_ref.shape[0] // t
    hp = pl.program_id(0)
    qi = pl.program_id(1)
    scale = HEAD_DIM ** -0.5
    heads = [_head_lanes((t, LANES), a) for a in range(2)]

    @pl.when(qi == 0)
    def _():
        def step(b, best):
            k = k_ref[pl.ds(pl.multiple_of(b * t, t), t), :].astype(F32)
            sq = k * k
            return tuple(jnp.maximum(best[a], jnp.sum(jnp.where(heads[a], sq, 0.0), axis=1, keepdims=True))
                         for a in range(2))
        zero = jnp.zeros((t, 1), F32)
        best = lax.fori_loop(0, nb, step, (zero, zero))
        for a in range(2):
            knorm_ref[a:a + 1, :] = jnp.broadcast_to(jnp.max(best[a], axis=0, keepdims=True), (1, LANES))

    r = lax.broadcasted_iota(jnp.int32, (t, t), 0)
    c = lax.broadcasted_iota(jnp.int32, (t, t), 1)
    causal = c <= r
    fl = _Flash(*flash_refs)
    fl.init(q_ref, scale)

    def block_logits(buf, j, mask=None):
        jc = jnp.maximum(j, 0)
        rows = pl.ds(pl.multiple_of(jc * t, t), t)

        def fix(a, z):
            cs = crow_ref[pl.ds(2 * hp + a, 1), rows]
            s = z - jnp.where(j >= 0, cs, -NEG_INF)
            return s if mask is None else jnp.where(mask, s, NEG_INF)
        fl.logits(buf, k_ref[rows, :], fix)

    def v_block(j):
        return v_ref[pl.ds(pl.multiple_of(jnp.maximum(j, 0) * t, t), t), :]

    block_logits(0, qi, causal)
    block_logits(1, qi - 1)
    fl.accumulate(0, v_block(qi))

    reach = []
    for a in range(2):
        q32 = fl.qs[a].astype(F32)
        qn2 = jnp.sum(q32 * q32, axis=1, keepdims=True)
        zmax = jnp.sqrt(qn2 * knorm_ref[a:a + 1, 0:1]) * 1.001
        reach.append(jnp.max(zmax - fl.m[a]))

    def alive(a, j):
        return reach[a] - cend_ref[(2 * hp + a) * nb + j] > EXP_ZERO - BOUND_SLACK

    def cond(j):
        jj = jnp.maximum(j, 0)
        return jnp.logical_and(j >= 0, jnp.logical_or(alive(0, jj), alive(1, jj)))

    def body(j):
        block_logits(0, j - 1)
        fl.accumulate(1, v_block(j))
        block_logits(1, j - 2)
        fl.accumulate(0, v_block(j - 1))
        return j - 2

    lax.while_loop(cond, body, qi - 1)
    o_ref[...] = fl.result().astype(o_ref.dtype)


def _attention(kind, qkv, out_cols, pair0, n_pairs, gates=None):
    S = qkv.shape[0]
    t = min(ATT_TILE, S)
    assert S % t == 0 and t == MOBA_BLOCK
    d_blocks = out_cols // LANES
    q_spec = pl.BlockSpec((t, LANES), lambda p, i: (i, pair0 + p))
    k_spec = pl.BlockSpec((S, LANES), lambda p, i: (0, d_blocks + pair0 + p))
    v_spec = pl.BlockSpec((S, LANES), lambda p, i: (0, 2 * d_blocks + pair0 + p))
    o_spec = pl.BlockSpec((t, LANES), lambda p, i: (i, p))
    in_specs = [q_spec, k_spec, v_spec]
    args = [qkv, qkv, qkv]
    scratch = []
    if kind == "sb":
        body = _sb_kernel
    elif kind == "moba":
        body = _moba_kernel
        nb = S // MOBA_BLOCK
        scratch = ([pltpu.VMEM((nb, LANES), F32)] + [pltpu.VMEM((nb, LANES), BF16)] * 3
                   + [pltpu.VMEM((2, t, nb), F32)] + _flash_scratch(t))
    else:
        body = _fox_kernel
        c_row = gates
        c_end = c_row[:, t - 1::t].reshape(-1)
        in_specs = [pl.BlockSpec(memory_space=pltpu.SMEM)] + in_specs + [
            pl.BlockSpec(c_row.shape, lambda p, i: (0, 0))]
        args = [c_end] + args + [c_row]
        scratch = [pltpu.VMEM((8, LANES), F32)] + _flash_scratch(t)
    return pl.pallas_call(
        body, grid=(n_pairs, S // t), in_specs=in_specs, out_specs=o_spec,
        out_shape=jax.ShapeDtypeStruct((S, n_pairs * LANES), BF16),
        scratch_shapes=scratch, compiler_params=_params("arbitrary", "arbitrary"), name=kind,
    )(*args)


def _memkv_kernel(mem_ref, g_ref, w_ref, o_ref):
    o_ref[...] = _dot(_rms(mem_ref[...], g_ref[...]).astype(BF16), w_ref[...]).astype(o_ref.dtype)


def _memory_kv(mem, g, w_kv):
    M, D = mem.shape
    N = w_kv.shape[1]
    return pl.pallas_call(
        _memkv_kernel, out_shape=jax.ShapeDtypeStruct((M, N), BF16), name="memkv",
        compiler_params=pltpu.CompilerParams(vmem_limit_bytes=VMEM_LIMIT),
    )(mem, g.reshape(1, D), w_kv.astype(BF16))


def _post_kernel(*refs, n_att):
    att_refs = refs[:n_att]
    h_ref, wo_ref, g_ref, wq_ref, kv_ref, wxo_ref, o_ref = refs[n_att:]
    tm = h_ref.shape[0]
    xa_dim = wq_ref.shape[1]
    hd = xa_dim // XA_HEADS
    h1 = h_ref[...]
    col = 0
    for a_ref in att_refs:
        w = a_ref.shape[1]
        h1 = h1 + _dot(a_ref[...], wo_ref[col:col + w, :])
        col += w
    q = _dot(_rms(h1, g_ref[...]).astype(BF16), wq_ref[...]).astype(BF16)
    kv = kv_ref[...]
    k = kv[:, :xa_dim]
    v = kv[:, xa_dim:]
    lane = lax.broadcasted_iota(jnp.int32, (tm, xa_dim), 1)
    o = jnp.zeros((tm, xa_dim), F32)
    for hh in range(XA_HEADS):
        mine = (lane >= hh * hd) & (lane < (hh + 1) * hd)
        s = _dot_nt(jnp.where(mine, q, jnp.zeros_like(q)), k) * (hd ** -0.5)
        m = jnp.max(s, axis=1, keepdims=True)
        p = jnp.exp(s - m)
        l = jnp.sum(p, axis=1, keepdims=True)
        o = jnp.where(mine, _dot(p.astype(BF16), v) / l, o)
    o_ref[...] = h1 + _dot(o.astype(BF16), wxo_ref[...])


def _post(att_parts, h, w_out, g_xa, w_q, kv, w_xo):
    S, D = h.shape
    tm = min(ROW_TILE, S)
    row = lambda i: (i, 0)
    in_specs = [pl.BlockSpec((tm, a.shape[1]), row) for a in att_parts]
    in_specs += [pl.BlockSpec((tm, D), row), _resident(w_out.shape), _resident((1, D)),
                 _resident(w_q.shape), _resident(kv.shape), _resident(w_xo.shape)]
    return pl.pallas_call(
        functools.partial(_post_kernel, n_att=len(att_parts)),
        grid=(S // tm,), in_specs=in_specs, out_specs=pl.BlockSpec((tm, D), row),
        out_shape=jax.ShapeDtypeStruct((S, D), F32), compiler_params=_params("arbitrary"), name="post",
    )(*att_parts, h, w_out.astype(BF16), g_xa.reshape(1, D), w_q.astype(BF16), kv, w_xo.astype(BF16))


def _ffn_kernel(*refs, d_ff, final):
    if final:
        h_ref, g_ref, wup_ref, cw_ref, cb_ref, wdn_ref, gfin_ref, o_ref, prev_ref = refs
    else:
        h_ref, g_ref, wup_ref, cw_ref, cb_ref, wdn_ref, o_ref, prev_ref = refs
    tm = h_ref.shape[0]

    @pl.when(pl.program_id(0) == 0)
    def _():
        prev_ref[...] = jnp.zeros_like(prev_ref)

    h = h_ref[...]
    hn = _rms(h, g_ref[...]).astype(BF16)
    rowi = lax.broadcasted_iota(jnp.int32, (tm, FF_CHUNK), 0)

    def conv(col0):
        cols = slice(col0, col0 + FF_CHUNK)
        u = _dot(hn, wup_ref[:, cols])
        prev = prev_ref[:, cols]
        p1, p2 = prev[7:8, :], prev[6:7, :]
        u1 = jnp.where(rowi == 0, p1, pltpu.roll(u, 1, 0))
        u2 = jnp.where(rowi == 0, p2, jnp.where(rowi == 1, p1, pltpu.roll(u, 2, 0)))
        prev_ref[:, cols] = u[tm - 8:tm, :]
        w = cw_ref[:, cols]
        return cb_ref[:, cols] + u2 * w[0:1, :] + u1 * w[1:2, :] + u * w[2:3, :]

    acc = h
    for c in range(d_ff // FF_CHUNK):
        gate = conv(c * FF_CHUNK)
        val = conv(d_ff + c * FF_CHUNK)
        act = gate * (1.0 / (1.0 + jnp.exp(-gate))) * val
        acc = acc + _dot(act.astype(BF16), wdn_ref[c * FF_CHUNK:(c + 1) * FF_CHUNK, :])
    if final:
        acc = _rms(acc, gfin_ref[...])
    o_ref[...] = acc


def _ffn(h, g, w_up, conv_w, conv_b, w_down, g_final=None):
    S, D = h.shape
    d_ff = w_down.shape[0]
    tm = min(ROW_TILE, S)
    assert d_ff % FF_CHUNK == 0 and conv_w.shape[0] == CONV_WIDTH
    row = lambda i: (i, 0)
    in_specs = [pl.BlockSpec((tm, D), row), _resident((1, D)), _resident(w_up.shape),
                _resident(conv_w.shape), _resident((1, 2 * d_ff)), _resident(w_down.shape)]
    args = [h, g.reshape(1, D), w_up.astype(BF16), conv_w, conv_b.reshape(1, 2 * d_ff), w_down.astype(BF16)]
    if g_final is not None:
        in_specs.append(_resident((1, D)))
        args.append(g_final.reshape(1, D))
    return pl.pallas_call(
        functools.partial(_ffn_kernel, d_ff=d_ff, final=g_final is not None),
        grid=(S // tm,), in_specs=in_specs, out_specs=pl.BlockSpec((tm, D), row),
        out_shape=jax.ShapeDtypeStruct((S, D), F32),
        scratch_shapes=[pltpu.VMEM((8, 2 * d_ff), F32)],
        compiler_params=_params("arbitrary"), name="ffn",
    )(*args)


def kernel(x, mem, positions, norm_mix_g, norm_xa_g, norm_mem_g, norm_ffn_g, ab_w_in, ab_w_out, fox_w_in, fox_b_f, fox_w_out, xa_w_q, xa_w_kv, xa_w_out, ffn_w_up, ffn_conv_w, ffn_conv_b, ffn_w_down, final_norm_g):
    B, S, D = x.shape
    depth = norm_mix_g.shape[0]
    n_pairs = D // LANES
    outs = []
    for b in range(B):
        h = x[b]
        for layer in range(depth):
            n = layer // 2
            if layer % 2 == 0:
                half_chunks = D // PROJ_CHUNK // 2
                rope_chunks = tuple(range(half_chunks, 2 * half_chunks)) + tuple(
                    range(3 * half_chunks, 4 * half_chunks))
                qkv = _project(h, norm_mix_g[layer], ab_w_in[n], rope=(positions[b], rope_chunks))
                att = [_attention("sb", qkv, D, 0, n_pairs // 2),
                       _attention("moba", qkv, D, n_pairs // 2, n_pairs // 2)]
                w_out = ab_w_out[n]
            else:
                w_in = fox_w_in[n]
                qkv, c_row = _project(h, norm_mix_g[layer], w_in[:, :3 * D],
                                      gates=(w_in[:, 3 * D:], fox_b_f[n]))
                att = [_attention("fox", qkv, D, 0, n_pairs, gates=c_row)]
                w_out = fox_w_out[n]
            kv = _memory_kv(mem[b], norm_mem_g[layer], xa_w_kv[layer])
            h = _post(att, h, w_out, norm_xa_g[layer], xa_w_q[layer], kv, xa_w_out[layer])
            h = _ffn(h, norm_ffn_g[layer], ffn_w_up[layer], ffn_conv_w[layer], ffn_conv_b[layer],
                     ffn_w_down[layer], final_norm_g if layer == depth - 1 else None)
        outs.append(h)
    return jnp.stack(outs)
```

```python
import functools

import numpy as np
import jax
import jax.numpy as jnp
from jax import lax
from jax.experimental import pallas as pl
from jax.experimental.pallas import tpu as pltpu

F32 = jnp.float32
BF16 = jnp.bfloat16

HEAD_DIM = 64
LANES = 128
MOBA_BLOCK = 256
MOBA_TOPK = 3
ROPE_THETA = 10000.0
XA_HEADS = 4
CONV_WIDTH = 3
RMS_EPS = 1e-6
NEG_INF = -1e9
KNOCKOUT = -3.0e38
LOG2E = float(np.log2(np.e))
EXP_ZERO = -104.0
BOUND_SLACK = 1.0
ATT_TILE = 256
KEY_TILE = 512
ROW_TILE = 512
PROJ_CHUNK = 512
FF_CHUNK = 256
VMEM_LIMIT = 56 * 1024 * 1024

_NT = (((1,), (1,)), ((), ()))


def _dot(a, b):
    return jnp.dot(a, b, preferred_element_type=F32)


def _dot_nt(a, b):
    return lax.dot_general(a, b, _NT, preferred_element_type=F32)


def _split2(x):
    hi = x.astype(BF16)
    lo = (x - hi.astype(F32)).astype(BF16)
    return hi, lo


def _split3(x):
    hi = x.astype(BF16)
    r = x - hi.astype(F32)
    mid = r.astype(BF16)
    lo = (r - mid.astype(F32)).astype(BF16)
    return hi, mid, lo


def _rms(x, g):
    return x * lax.rsqrt(jnp.mean(x * x, axis=-1, keepdims=True) + RMS_EPS) * g


def _log_sigmoid(z):
    return jnp.minimum(z, 0.0) - jnp.log(1.0 + jnp.exp(-jnp.abs(z)))


def _resident(shape):
    return pl.BlockSpec(shape, lambda *_: (0,) * len(shape), pipeline_mode=pl.Buffered(1))


def _params(*sem):
    return pltpu.CompilerParams(dimension_semantics=sem, vmem_limit_bytes=VMEM_LIMIT)


def _proj_kernel(*refs, chunk_scale, rope_chunks, key_chunk, with_gates):
    if with_gates:
        x_ref, g_ref, w_ref, wft_ref, bfc_ref, o_ref, crow_ref, carry_row = refs
    elif rope_chunks:
        x_ref, g_ref, w_ref, pos_ref, o_ref, kaug_ref = refs
    else:
        x_ref, g_ref, w_ref, o_ref = refs
    n_chunks = len(chunk_scale)
    tm = x_ref.shape[0]
    hn32 = _rms(x_ref[...], g_ref[...])
    hn = hn32.astype(BF16)

    if rope_chunks:
        half = HEAD_DIM // 2
        lane = lax.broadcasted_iota(jnp.int32, (tm, LANES), 1)
        first_half = (lane % HEAD_DIM) < half
        freq_id = (lax.broadcasted_iota(jnp.int32, (1, LANES), 1) % half).astype(F32)
        inv_freq = jnp.exp(freq_id * (-np.log(ROPE_THETA) / half))
        ang = pos_ref[...].astype(F32) * inv_freq
        cos, sin = jnp.cos(ang), jnp.sin(ang)
        sin_signed = jnp.where(first_half, -sin, sin)

    for c in range(n_chunks):
        y = _dot(hn, w_ref[:, c * PROJ_CHUNK:(c + 1) * PROJ_CHUNK])
        if c in rope_chunks:
            parts = []
            for s in range(PROJ_CHUNK // LANES):
                ys = y[:, s * LANES:(s + 1) * LANES]
                rot = jnp.where(first_half, pltpu.roll(ys, LANES - HEAD_DIM // 2, 1),
                                pltpu.roll(ys, HEAD_DIM // 2, 1))
                parts.append(ys * cos + rot * sin_signed)
            if c == key_chunk:
                row = pl.program_id(0) * tm + lax.broadcasted_iota(jnp.int32, (tm, LANES), 0)
                one_hot = (lane - HEAD_DIM == row // MOBA_BLOCK).astype(F32)
                for s, ys in enumerate(parts):
                    for a, head in enumerate((ys, pltpu.roll(ys, HEAD_DIM, 1))):
                        col = (2 * s + a) * LANES
                        kaug_ref[:, col:col + LANES] = jnp.where(lane < HEAD_DIM, head, one_hot).astype(BF16)
            y = jnp.concatenate(parts, axis=1)
        if chunk_scale[c] != 1.0:
            y = y * chunk_scale[c]
        o_ref[:, c * PROJ_CHUNK:(c + 1) * PROJ_CHUNK] = y.astype(o_ref.dtype)

    if with_gates:
        @pl.when(pl.program_id(0) == 0)
        def _():
            carry_row[...] = jnp.zeros_like(carry_row)

        h_hi, h_lo = _split2(hn32)
        wt_hi, wt_lo = _split2(wft_ref[...])
        f_row = _dot_nt(wt_hi, h_hi) + _dot_nt(wt_hi, h_lo) + _dot_nt(wt_lo, h_hi) + bfc_ref[...]
        lf_row = _log_sigmoid(f_row)
        r = lax.broadcasted_iota(jnp.int32, (tm, tm), 0)
        cidx = lax.broadcasted_iota(jnp.int32, (tm, tm), 1)
        incl_t = (r <= cidx).astype(BF16)
        c_row = carry_row[...]
        for piece in _split3(lf_row):
            c_row = c_row + _dot(piece, incl_t)
        crow_ref[...] = c_row * LOG2E
        carry_row[...] = c_row[:, tm - 1:tm]


def _project(x, g, w, chunk_scale, *, out_dtype=BF16, rope=None, gates=None):
    S, D = x.shape
    N = w.shape[1]
    tm = min(ROW_TILE, S)
    assert S % tm == 0 and N == PROJ_CHUNK * len(chunk_scale)
    row = lambda i: (i, 0)
    in_specs = [pl.BlockSpec((tm, D), row), _resident((1, D)), _resident((D, N))]
    args = [x, g.reshape(1, D), w.astype(BF16)]
    out_shape = [jax.ShapeDtypeStruct((S, N), out_dtype)]
    out_specs = [pl.BlockSpec((tm, N), row)]
    scratch = []
    rope_chunks, key_chunk = (), None
    if rope is not None:
        positions, rope_chunks, key_chunk = rope
        assert S // MOBA_BLOCK <= HEAD_DIM
        in_specs += [pl.BlockSpec((tm, 1), row)]
        args += [positions.reshape(S, 1)]
        out_shape += [jax.ShapeDtypeStruct((S, 2 * PROJ_CHUNK), BF16)]
        out_specs += [pl.BlockSpec((tm, 2 * PROJ_CHUNK), row)]
    if gates is not None:
        w_f, b_f = gates
        H = w_f.shape[1]
        in_specs += [_resident((H, D)), _resident((H, 1))]
        args += [w_f.T, b_f.reshape(H, 1)]
        out_shape += [jax.ShapeDtypeStruct((H, S), F32)]
        out_specs += [pl.BlockSpec((H, tm), lambda i: (0, i))]
        scratch = [pltpu.VMEM((H, 1), F32)]
    outs = pl.pallas_call(
        functools.partial(_proj_kernel, chunk_scale=tuple(chunk_scale), rope_chunks=tuple(rope_chunks),
                          key_chunk=key_chunk, with_gates=gates is not None),
        grid=(S // tm,), in_specs=in_specs, out_specs=out_specs, out_shape=out_shape,
        scratch_shapes=scratch, compiler_params=_params("arbitrary"),
        name="proj_gates" if gates is not None else ("proj_rope" if rope is not None else "proj"),
    )(*args)
    return outs


def _head_lanes(shape, a):
    lane = lax.broadcasted_iota(jnp.int32, shape, len(shape) - 1)
    return (lane >= a * HEAD_DIM) & (lane < (a + 1) * HEAD_DIM)


def _pair_queries(q_ref, scale):
    q2 = q_ref[...] * jnp.asarray(scale, q_ref.dtype)
    return [jnp.where(_head_lanes(q2.shape, a), q2, jnp.zeros_like(q2)) for a in range(2)]


def _kv_block(k_ref, v_ref, j, t):
    rows = pl.ds(pl.multiple_of(j * t, t), t)
    return k_ref[rows, :], v_ref[rows, :]


def _sb_kernel(q_ref, k_ref, v_ref, o_ref):
    t = q_ref.shape[0]
    qi = pl.program_id(1)
    scale = HEAD_DIM ** -0.5
    r = lax.broadcasted_iota(jnp.int32, (t, t), 0)
    c = lax.broadcasted_iota(jnp.int32, (t, t), 1)
    strict = c < r
    later = (r > c).astype(BF16)
    qs = _pair_queries(q_ref, scale)
    pair = range(2)

    def inner(k, diagonal):
        z = [_dot_nt(qs[a], k) for a in pair]
        lb = [_log_sigmoid(z[a]) for a in pair]
        l1 = [lb[a] - z[a] for a in pair]
        if diagonal:
            l1 = [jnp.where(strict, l1[a], 0.0) for a in pair]
        parts = [_split2(l1[a]) for a in pair]
        tail = [_dot(parts[a][0], later) + _dot(parts[a][1], later) for a in pair]
        u = [lb[a] + tail[a] for a in pair]
        if diagonal:
            u = [jnp.where(strict, u[a], NEG_INF) for a in pair]
        return u, [tail[a][:, 0:1] + l1[a][:, 0:1] for a in pair]

    def weigh(u, carry, v):
        return [_dot(jnp.exp(u[a] + carry[a]).astype(BF16), v) for a in pair]

    k_own, v_own = _kv_block(k_ref, v_ref, qi, t)
    k_prev, v_prev = _kv_block(k_ref, v_ref, jnp.maximum(qi - 1, 0), t)
    v_prev = v_prev * jnp.where(qi > 0, 1.0, 0.0).astype(v_prev.dtype)
    u_own, sum_own = inner(k_own, True)
    u_prev, sum_prev = inner(k_prev, False)
    acc = weigh(u_own, [0.0, 0.0], v_own)
    pv = weigh(u_prev, sum_own, v_prev)
    acc = [acc[a] + pv[a] for a in pair]
    carry = [sum_own[a] + sum_prev[a] for a in pair]

    def cond(st):
        kb, carry0, carry1, _, _ = st
        return jnp.logical_and(kb >= 0, jnp.max(jnp.maximum(carry0, carry1)) > EXP_ZERO)

    def body(st):
        kb, carry0, carry1, acc0, acc1 = st
        k, v = _kv_block(k_ref, v_ref, kb, t)
        u, sums = inner(k, False)
        pv = weigh(u, [carry0, carry1], v)
        return kb - 1, carry0 + sums[0], carry1 + sums[1], acc0 + pv[0], acc1 + pv[1]

    _, _, _, acc0, acc1 = lax.while_loop(cond, body, (qi - 2, carry[0], carry[1], acc[0], acc[1]))
    o_ref[...] = jnp.where(_head_lanes((t, LANES), 0), acc0, acc1).astype(o_ref.dtype)


class _Flash:
    def __init__(self, qs, s, rm, m, l, acc):
        self.qs, self.s, self.rm, self.m, self.l, self.acc = qs, s, rm, m, l, acc

    def init(self):
        for a in range(2):
            self.m[a] = jnp.full(self.m.shape[1:], KNOCKOUT, F32)
            self.l[a] = jnp.zeros(self.l.shape[1:], F32)
            self.acc[a] = jnp.zeros(self.acc.shape[1:], F32)

    def store_logits(self, buf, a, s):
        self.s[2 * buf + a] = s
        self.rm[2 * buf + a] = jnp.broadcast_to(jnp.max(s, axis=1, keepdims=True), self.rm.shape[1:])

    def accumulate(self, buf, v):
        for a in range(2):
            m_new = jnp.maximum(self.m[a], self.rm[2 * buf + a])
            alpha = jnp.exp2(self.m[a] - m_new)
            s = self.s[2 * buf + a]
            p = jnp.exp2(s - jnp.tile(m_new, (1, s.shape[1] // LANES)))
            part = p[:, :LANES]
            for c in range(1, s.shape[1] // LANES):
                part = part + p[:, c * LANES:(c + 1) * LANES]
            self.l[a] = alpha * self.l[a] + part
            self.acc[a] = alpha * self.acc[a] + _dot(p.astype(BF16), v)
            self.m[a] = m_new

    def result(self):
        t = self.acc.shape[1]
        out = [self.acc[a] / jnp.sum(self.l[a], axis=1, keepdims=True) for a in range(2)]
        return jnp.where(_head_lanes((t, LANES), 0), out[0], out[1])


def _flash_scratch(t, kt):
    return [pltpu.VMEM((4, t, LANES), BF16), pltpu.VMEM((4, t, kt), F32), pltpu.VMEM((4, t, LANES), F32),
            pltpu.VMEM((2, t, LANES), F32), pltpu.VMEM((2, t, LANES), F32), pltpu.VMEM((2, t, LANES), F32)]


def _moba_kernel(q_ref, k_ref, v_ref, o_ref, km32, km_hi, km_mid, km_lo, *flash_refs):
    t = q_ref.shape[0]
    nb = k_ref.shape[0] // MOBA_BLOCK
    qi = pl.program_id(1)
    fl = _Flash(*flash_refs)
    kt = fl.s.shape[2]
    n_tiles = k_ref.shape[0] // kt
    n_past = (qi * t) // kt

    @pl.when(qi == 0)
    def _():
        km32[...] = jnp.zeros_like(km32)

        def centroid(b, _):
            blk = k_ref[pl.ds(pl.multiple_of(b * MOBA_BLOCK, MOBA_BLOCK), MOBA_BLOCK), :].astype(F32)
            km32[pl.ds(HEAD_DIM + b, 1), :] = jnp.sum(blk, axis=0, keepdims=True) * (1.0 / MOBA_BLOCK)
            return 0
        lax.fori_loop(0, nb, centroid, 0)
        hi, mid, lo = _split3(km32[...])
        km_hi[...] = hi
        km_mid[...] = mid
        km_lo[...] = lo

    lane = lax.broadcasted_iota(jnp.int32, (t, LANES), 1)
    low = lane < HEAD_DIM
    blk = lane - HEAD_DIM
    lane_f = lane.astype(F32)
    q2 = q_ref[...]
    q_heads = (q2, pltpu.roll(q2.astype(F32), HEAD_DIM, 1).astype(q2.dtype))
    for a in range(2):
        q = jnp.where(low, q_heads[a], jnp.zeros_like(q2))
        cols = slice(a * LANES, (a + 1) * LANES)
        gate = _dot_nt(q, km_hi[:, cols]) + _dot_nt(q, km_mid[:, cols]) + _dot_nt(q, km_lo[:, cols])
        g = jnp.where(jnp.logical_and(blk >= 0, blk < qi), gate, NEG_INF)
        sel = jnp.zeros((t, LANES), F32)
        for rank in range(MOBA_TOPK):
            best = jnp.max(g, axis=1, keepdims=True)
            idx = jnp.min(jnp.where(g == best, lane_f, float(LANES)), axis=1, keepdims=True)
            pick = lane_f == idx
            sel = jnp.where(pick, jnp.where(rank < qi, 1.0, 0.0), sel)
            g = jnp.where(pick, KNOCKOUT, g)
        past_bias = jnp.where(sel > 0.0, 0.0, NEG_INF).astype(q2.dtype)
        own_bias = jnp.where(jnp.logical_or(sel > 0.0, blk == qi), 0.0, NEG_INF).astype(q2.dtype)
        fl.qs[a] = jnp.where(low, q, past_bias)
        fl.qs[2 + a] = jnp.where(low, q, own_bias)

    def rows(j):
        return pl.ds(pl.multiple_of(jnp.minimum(j, n_tiles - 1) * kt, kt), kt)

    def tile_logits(buf, j, own=False):
        for a in range(2):
            s = _dot_nt(fl.qs[2 * own + a], k_ref[rows(j), a * LANES:(a + 1) * LANES])
            fl.store_logits(buf, a, jnp.where(causal, s, NEG_INF) if own else s)

    r = lax.broadcasted_iota(jnp.int32, (t, kt), 0)
    c = lax.broadcasted_iota(jnp.int32, (t, kt), 1)
    causal = c <= r + (qi * t - n_past * kt)

    fl.init()
    tile_logits(0, n_past, own=True)
    tile_logits(1, 0)
    fl.accumulate(0, v_ref[rows(n_past), :])

    def body(i, _):
        tile_logits(0, 2 * i + 1)
        fl.accumulate(1, v_ref[rows(2 * i), :])
        tile_logits(1, 2 * i + 2)
        fl.accumulate(0, v_ref[rows(2 * i + 1), :])
        return 0

    lax.fori_loop(0, n_past // 2, body, 0)

    @pl.when(n_past % 2 == 1)
    def _():
        fl.accumulate(1, v_ref[rows(n_past - 1), :])

    o_ref[...] = fl.result().astype(o_ref.dtype)


def _fox_kernel(cend_ref, q_ref, k_ref, v_ref, crow_ref, o_ref, knorm_ref, *flash_refs):
    t = q_ref.shape[0]
    hp = pl.program_id(0)
    qi = pl.program_id(1)
    fl = _Flash(*flash_refs)
    kt = fl.s.shape[2]
    n_tiles = k_ref.shape[0] // kt
    n_past = (qi * t) // kt

    @pl.when(qi == 0)
    def _():
        def step(b, best):
            k = k_ref[pl.ds(pl.multiple_of(b * kt, kt), kt), :].astype(F32)
            sq = k * k
            norms = [jnp.sum(jnp.where(_head_lanes((1, LANES), a), sq, 0.0), axis=1, keepdims=True)
                     for a in range(2)]
            return tuple(jnp.maximum(best[a], norms[a]) for a in range(2))
        zero = jnp.zeros((kt, 1), F32)
        best = lax.fori_loop(0, n_tiles, step, (zero, zero))
        for a in range(2):
            knorm_ref[a:a + 1, :] = jnp.broadcast_to(jnp.max(best[a], axis=0, keepdims=True), (1, LANES))

    q2 = q_ref[...]
    for a in range(2):
        fl.qs[a] = jnp.where(_head_lanes((t, LANES), a), q2, jnp.zeros_like(q2))

    def rows(j):
        return pl.ds(pl.multiple_of(jnp.maximum(j, 0) * kt, kt), kt)

    def tile_logits(buf, j, diagonal=False):
        k = k_ref[rows(j), :]
        for a in range(2):
            cs = crow_ref[pl.ds(2 * hp + a, 1), rows(j)]
            s = _dot_nt(fl.qs[a], k) - cs
            fl.store_logits(buf, a, jnp.where(causal, s, NEG_INF) if diagonal else s)

    r = lax.broadcasted_iota(jnp.int32, (t, kt), 0)
    c = lax.broadcasted_iota(jnp.int32, (t, kt), 1)
    causal = c <= r + (qi * t - n_past * kt)

    fl.init()
    tile_logits(0, n_past, diagonal=True)
    tile_logits(1, n_past - 1)
    fl.accumulate(0, v_ref[rows(n_past), :])

    reach = []
    for a in range(2):
        q32 = fl.qs[a].astype(F32)
        qn2 = jnp.sum(q32 * q32, axis=1, keepdims=True)
        zmax = jnp.sqrt(qn2 * knorm_ref[a:a + 1, 0:1]) * 1.001
        reach.append(jnp.max(zmax - fl.m[a]))

    def alive(j):
        jj = jnp.maximum(j, 0)
        live = [reach[a] - cend_ref[(2 * hp + a) * n_tiles + jj] > (EXP_ZERO - BOUND_SLACK) * LOG2E
                for a in range(2)]
        return jnp.logical_and(j >= 0, jnp.logical_or(live[0], live[1]))

    def body(j):
        tile_logits(0, j - 1)
        fl.accumulate(1, v_ref[rows(j), :])
        tile_logits(1, j - 2)
        fl.accumulate(0, v_ref[rows(j - 1), :])
        return j - 2

    j = lax.while_loop(lambda j: alive(j - 1), body, n_past - 1)

    @pl.when(alive(j))
    def _():
        fl.accumulate(1, v_ref[rows(j), :])

    o_ref[...] = fl.result().astype(o_ref.dtype)


def _attention(kind, q_arr, q_block0, k_arr, k_block0, v_arr, v_block0, n_pairs, gates=None):
    S = q_arr.shape[0]
    t = min(ATT_TILE, S)
    kt = min(KEY_TILE, S)
    assert S % kt == 0 and kt % t == 0 and t == MOBA_BLOCK
    k_lanes = 2 * LANES if kind == "moba" else LANES
    q_spec = pl.BlockSpec((t, LANES), lambda p, i: (i, q_block0 + p))
    k_spec = pl.BlockSpec((S, k_lanes), lambda p, i: (0, k_block0 + p))
    v_spec = pl.BlockSpec((S, LANES), lambda p, i: (0, v_block0 + p))
    o_spec = pl.BlockSpec((t, LANES), lambda p, i: (i, p))
    in_specs = [q_spec, k_spec, v_spec]
    args = [q_arr, k_arr, v_arr]
    scratch = []
    if kind == "sb":
        body = _sb_kernel
    elif kind == "moba":
        body = _moba_kernel
        assert S // MOBA_BLOCK <= HEAD_DIM
        scratch = ([pltpu.VMEM((LANES, 2 * LANES), F32)] + [pltpu.VMEM((LANES, 2 * LANES), BF16)] * 3
                   + _flash_scratch(t, kt))
    else:
        body = _fox_kernel
        c_row = gates
        c_end = c_row[:, kt - 1::kt].reshape(-1)
        in_specs = [pl.BlockSpec(memory_space=pltpu.SMEM)] + in_specs + [
            pl.BlockSpec(c_row.shape, lambda p, i: (0, 0))]
        args = [c_end] + args + [c_row]
        scratch = [pltpu.VMEM((8, LANES), F32)] + _flash_scratch(t, kt)
    return pl.pallas_call(
        body, grid=(n_pairs, S // t), in_specs=in_specs, out_specs=o_spec,
        out_shape=jax.ShapeDtypeStruct((S, n_pairs * LANES), BF16),
        scratch_shapes=scratch, compiler_params=_params("arbitrary", "arbitrary"), name=kind,
    )(*args)


def _memkv_kernel(mem_ref, g_ref, w_ref, o_ref):
    o_ref[...] = _dot(_rms(mem_ref[...], g_ref[...]).astype(BF16), w_ref[...]).astype(o_ref.dtype)


def _memory_kv(mem, g, w_kv):
    M, D = mem.shape
    N = w_kv.shape[1]
    return pl.pallas_call(
        _memkv_kernel, out_shape=jax.ShapeDtypeStruct((M, N), BF16), name="memkv",
        compiler_params=pltpu.CompilerParams(vmem_limit_bytes=VMEM_LIMIT),
    )(mem, g.reshape(1, D), w_kv.astype(BF16))


def _post_kernel(*refs, n_att):
    att_refs = refs[:n_att]
    h_ref, wo_ref, g_ref, wq_ref, kv_ref, wxo_ref, o_ref = refs[n_att:]
    tm = h_ref.shape[0]
    xa_dim = wq_ref.shape[1]
    hd = xa_dim // XA_HEADS
    h1 = h_ref[...]
    col = 0
    for a_ref in att_refs:
        w = a_ref.shape[1]
        h1 = h1 + _dot(a_ref[...], wo_ref[col:col + w, :])
        col += w
    q = _dot(_rms(h1, g_ref[...]).astype(BF16), wq_ref[...]).astype(BF16)
    kv = kv_ref[...]
    k = kv[:, :xa_dim]
    v = kv[:, xa_dim:]
    lane = lax.broadcasted_iota(jnp.int32, (tm, xa_dim), 1)
    o = jnp.zeros((tm, xa_dim), F32)
    for hh in range(XA_HEADS):
        mine = (lane >= hh * hd) & (lane < (hh + 1) * hd)
        s = _dot_nt(jnp.where(mine, q, jnp.zeros_like(q)), k) * (hd ** -0.5)
        m = jnp.max(s, axis=1, keepdims=True)
        p = jnp.exp(s - m)
        l = jnp.sum(p, axis=1, keepdims=True)
        o = jnp.where(mine, _dot(p.astype(BF16), v) / l, o)
    o_ref[...] = h1 + _dot(o.astype(BF16), wxo_ref[...])


def _post(att_parts, h, w_out, g_xa, w_q, kv, w_xo):
    S, D = h.shape
    tm = min(ROW_TILE, S)
    row = lambda i: (i, 0)
    in_specs = [pl.BlockSpec((tm, a.shape[1]), row) for a in att_parts]
    in_specs += [pl.BlockSpec((tm, D), row), _resident(w_out.shape), _resident((1, D)),
                 _resident(w_q.shape), _resident(kv.shape), _resident(w_xo.shape)]
    return pl.pallas_call(
        functools.partial(_post_kernel, n_att=len(att_parts)),
        grid=(S // tm,), in_specs=in_specs, out_specs=pl.BlockSpec((tm, D), row),
        out_shape=jax.ShapeDtypeStruct((S, D), F32), compiler_params=_params("arbitrary"), name="post",
    )(*att_parts, h, w_out.astype(BF16), g_xa.reshape(1, D), w_q.astype(BF16), kv, w_xo.astype(BF16))


def _ffn_kernel(*refs, d_ff, final):
    if final:
        h_ref, g_ref, wup_ref, cw_ref, cb_ref, wdn_ref, gfin_ref, o_ref, prev_ref = refs
    else:
        h_ref, g_ref, wup_ref, cw_ref, cb_ref, wdn_ref, o_ref, prev_ref = refs
    tm = h_ref.shape[0]

    @pl.when(pl.program_id(0) == 0)
    def _():
        prev_ref[...] = jnp.zeros_like(prev_ref)

    h = h_ref[...]
    hn = _rms(h, g_ref[...]).astype(BF16)
    rowi = lax.broadcasted_iota(jnp.int32, (tm, FF_CHUNK), 0)

    def conv(col0):
        cols = slice(col0, col0 + FF_CHUNK)
        u = _dot(hn, wup_ref[:, cols])
        prev = prev_ref[:, cols]
        p1, p2 = prev[7:8, :], prev[6:7, :]
        u1 = jnp.where(rowi == 0, p1, pltpu.roll(u, 1, 0))
        u2 = jnp.where(rowi == 0, p2, jnp.where(rowi == 1, p1, pltpu.roll(u, 2, 0)))
        prev_ref[:, cols] = u[tm - 8:tm, :]
        w = cw_ref[:, cols]
        return cb_ref[:, cols] + u2 * w[0:1, :] + u1 * w[1:2, :] + u * w[2:3, :]

    acc = h
    for c in range(d_ff // FF_CHUNK):
        gate = conv(c * FF_CHUNK)
        val = conv(d_ff + c * FF_CHUNK)
        act = gate * (1.0 / (1.0 + jnp.exp(-gate))) * val
        acc = acc + _dot(act.astype(BF16), wdn_ref[c * FF_CHUNK:(c + 1) * FF_CHUNK, :])
    if final:
        acc = _rms(acc, gfin_ref[...])
    o_ref[...] = acc


def _ffn(h, g, w_up, conv_w, conv_b, w_down, g_final=None):
    S, D = h.shape
    d_ff = w_down.shape[0]
    tm = min(ROW_TILE, S)
    assert d_ff % FF_CHUNK == 0 and conv_w.shape[0] == CONV_WIDTH
    row = lambda i: (i, 0)
    in_specs = [pl.BlockSpec((tm, D), row), _resident((1, D)), _resident(w_up.shape),
                _resident(conv_w.shape), _resident((1, 2 * d_ff)), _resident(w_down.shape)]
    args = [h, g.reshape(1, D), w_up.astype(BF16), conv_w, conv_b.reshape(1, 2 * d_ff), w_down.astype(BF16)]
    if g_final is not None:
        in_specs.append(_resident((1, D)))
        args.append(g_final.reshape(1, D))
    return pl.pallas_call(
        functools.partial(_ffn_kernel, d_ff=d_ff, final=g_final is not None),
        grid=(S // tm,), in_specs=in_specs, out_specs=pl.BlockSpec((tm, D), row),
        out_shape=jax.ShapeDtypeStruct((S, D), F32),
        scratch_shapes=[pltpu.VMEM((8, 2 * d_ff), F32)],
        compiler_params=_params("arbitrary"), name="ffn",
    )(*args)


def kernel(x, mem, positions, norm_mix_g, norm_xa_g, norm_mem_g, norm_ffn_g, ab_w_in, ab_w_out, fox_w_in, fox_b_f, fox_w_out, xa_w_q, xa_w_kv, xa_w_out, ffn_w_up, ffn_conv_w, ffn_conv_b, ffn_w_down, final_norm_g):
    B, S, D = x.shape
    depth = norm_mix_g.shape[0]
    n_pairs = D // LANES
    q_scale2 = HEAD_DIM ** -0.5 * LOG2E
    outs = []
    for b in range(B):
        h = x[b]
        for layer in range(depth):
            n = layer // 2
            if layer % 2 == 0:
                assert D == 2 * PROJ_CHUNK
                half = n_pairs // 2
                qkv, k_moba = _project(h, norm_mix_g[layer], ab_w_in[n], (1.0, q_scale2, 1.0, 1.0, 1.0, 1.0),
                                       rope=(positions[b], (1, 3), 3))
                att = [_attention("sb", qkv, 0, qkv, n_pairs, qkv, 2 * n_pairs, half),
                       _attention("moba", qkv, half, k_moba, 0, qkv, 2 * n_pairs + half, half)]
                w_out = ab_w_out[n]
            else:
                w_in = fox_w_in[n]
                qkv, c_row = _project(h, norm_mix_g[layer], w_in[:, :3 * D],
                                      (q_scale2, q_scale2, 1.0, 1.0, 1.0, 1.0),
                                      gates=(w_in[:, 3 * D:], fox_b_f[n]))
                att = [_attention("fox", qkv, 0, qkv, n_pairs, qkv, 2 * n_pairs, n_pairs, gates=c_row)]
                w_out = fox_w_out[n]
            kv = _memory_kv(mem[b], norm_mem_g[layer], xa_w_kv[layer])
            h = _post(att, h, w_out, norm_xa_g[layer], xa_w_q[layer], kv, xa_w_out[layer])
            h = _ffn(h, norm_ffn_g[layer], ffn_w_up[layer], ffn_conv_w[layer], ffn_conv_b[layer],
                     ffn_w_down[layer], final_norm_g if layer == depth - 1 else None)
        outs.append(h)
    return jnp.stack(outs)
```

```python
import functools

import numpy as np
import jax
import jax.numpy as jnp
from jax import lax
from jax.experimental import pallas as pl
from jax.experimental.pallas import tpu as pltpu

F32 = jnp.float32
BF16 = jnp.bfloat16

HEAD_DIM = 64
LANES = 128
MOBA_BLOCK = 256
MOBA_TOPK = 3
ROPE_THETA = 10000.0
XA_HEADS = 4
CONV_WIDTH = 3
RMS_EPS = 1e-6
NEG_INF = -1e9
KNOCKOUT = -3.0e38
LOG2E = float(np.log2(np.e))
EXP_ZERO = -104.0
BOUND_SLACK = 1.0
ATT_TILE = 256
KEY_TILE = {"moba": 1024, "fox": 512}
ROW_TILE = 512
PROJ_CHUNK = 512
FF_CHUNK = 256
VMEM_LIMIT = 56 * 1024 * 1024

_NT = (((1,), (1,)), ((), ()))


def _dot(a, b):
    return jnp.dot(a, b, preferred_element_type=F32)


def _dot_nt(a, b):
    return lax.dot_general(a, b, _NT, preferred_element_type=F32)


def _split2(x):
    hi = x.astype(BF16)
    lo = (x - hi.astype(F32)).astype(BF16)
    return hi, lo


def _split3(x):
    hi = x.astype(BF16)
    r = x - hi.astype(F32)
    mid = r.astype(BF16)
    lo = (r - mid.astype(F32)).astype(BF16)
    return hi, mid, lo


def _rms(x, g):
    return x * lax.rsqrt(jnp.mean(x * x, axis=-1, keepdims=True) + RMS_EPS) * g


def _log_sigmoid(z):
    return jnp.minimum(z, 0.0) - jnp.log(1.0 + jnp.exp(-jnp.abs(z)))


def _resident(shape):
    return pl.BlockSpec(shape, lambda *_: (0,) * len(shape), pipeline_mode=pl.Buffered(1))


def _params(*sem):
    return pltpu.CompilerParams(dimension_semantics=sem, vmem_limit_bytes=VMEM_LIMIT)


def _proj_kernel(*refs, chunk_scale, rope_chunks, out_chunks, key_chunks, gate_keys, n_vt):
    refs = list(refs)
    x_ref, g_ref, w_ref = refs[:3]
    del refs[:3]
    pos_ref = refs.pop(0) if rope_chunks else None
    wvt_ref = refs.pop(0)
    if gate_keys:
        wf_ref, bf_ref, place_ref = refs[:3]
        del refs[:3]
    o_ref, kaug_ref, vt_ref = refs[:3]
    del refs[:3]
    if gate_keys:
        ccol_ref, carry_ref = refs
    tm = x_ref.shape[0]
    hn32 = _rms(x_ref[...], g_ref[...])
    hn = hn32.astype(BF16)
    lane = lax.broadcasted_iota(jnp.int32, (tm, LANES), 1)
    low = lane < HEAD_DIM

    if rope_chunks:
        half = HEAD_DIM // 2
        first_half = (lane % HEAD_DIM) < half
        freq_id = (lax.broadcasted_iota(jnp.int32, (1, LANES), 1) % half).astype(F32)
        inv_freq = jnp.power(ROPE_THETA, -freq_id / half)
        ang = pos_ref[...].astype(F32) * inv_freq
        cos, sin = jnp.cos(ang), jnp.sin(ang)
        sin_signed = jnp.where(first_half, -sin, sin)

    if gate_keys:
        @pl.when(pl.program_id(0) == 0)
        def _():
            carry_ref[...] = jnp.zeros_like(carry_ref)

        h_hi, h_lo = _split2(hn32)
        w_hi, w_lo = _split2(wf_ref[...])
        f = _dot(h_hi, w_hi) + _dot(h_lo, w_hi) + _dot(h_hi, w_lo) + bf_ref[...]
        r = lax.broadcasted_iota(jnp.int32, (tm, tm), 0)
        cidx = lax.broadcasted_iota(jnp.int32, (tm, tm), 1)
        incl = (cidx <= r).astype(BF16)
        c_col = carry_ref[...]
        for piece in _split3(_log_sigmoid(f)):
            c_col = c_col + _dot(incl, piece)
        carry_ref[...] = c_col[tm - 1:tm, :]
        c2 = c_col * LOG2E
        ccol_ref[...] = c2
        extras = sum(_dot(piece, place_ref[i]) for i, piece in enumerate(_split3(c2)))
    elif key_chunks:
        row = pl.program_id(0) * tm + lax.broadcasted_iota(jnp.int32, (tm, LANES), 0)
        one_hot = (lane - HEAD_DIM == row // MOBA_BLOCK).astype(F32)

    for c in range(len(chunk_scale)):
        y = _dot(hn, w_ref[:, c * PROJ_CHUNK:(c + 1) * PROJ_CHUNK])
        parts = [y[:, s * LANES:(s + 1) * LANES] for s in range(PROJ_CHUNK // LANES)]
        if c in rope_chunks:
            parts = [ys * cos + sin_signed * jnp.where(first_half, pltpu.roll(ys, LANES - HEAD_DIM // 2, 1),
                                                       pltpu.roll(ys, HEAD_DIM // 2, 1)) for ys in parts]
        if c in key_chunks:
            base = key_chunks.index(c) * 2 * len(parts)
            for s, ys in enumerate(parts):
                for a, head in enumerate((ys, pltpu.roll(ys, HEAD_DIM, 1))):
                    col = (base + 2 * s + a) * LANES
                    upper = extras[:, col:col + LANES] if gate_keys else one_hot
                    kaug_ref[:, col:col + LANES] = jnp.where(low, head, upper).astype(BF16)
        if c in out_chunks:
            y = jnp.concatenate(parts, axis=1)
            if chunk_scale[c] != 1.0:
                y = y * chunk_scale[c]
            col = out_chunks.index(c) * PROJ_CHUNK
            o_ref[:, col:col + PROJ_CHUNK] = y.astype(o_ref.dtype)

    for i in range(n_vt // PROJ_CHUNK):
        rows = slice(i * PROJ_CHUNK, (i + 1) * PROJ_CHUNK)
        vt_ref[rows, :] = _dot_nt(wvt_ref[rows, :], hn).astype(BF16)


def _project(x, g, w, w_vt, chunk_scale, out_chunks, key_chunks, *, rope=None, gates=None):
    S, D = x.shape
    tm = min(ROW_TILE, S)
    n_vt = w_vt.shape[0]
    n_key_lanes = len(key_chunks) * 2 * PROJ_CHUNK
    assert S % tm == 0 and w.shape[1] == PROJ_CHUNK * len(chunk_scale) and n_vt % PROJ_CHUNK == 0
    row = lambda i: (i, 0)
    in_specs = [pl.BlockSpec((tm, D), row), _resident((1, D)), _resident(w.shape)]
    args = [x, g.reshape(1, D), w.astype(BF16)]
    rope_chunks = ()
    if rope is not None:
        positions, rope_chunks = rope
        assert S // MOBA_BLOCK <= HEAD_DIM
        in_specs += [pl.BlockSpec((tm, 1), row)]
        args += [positions.reshape(S, 1)]
    in_specs += [_resident(w_vt.shape)]
    args += [w_vt.astype(BF16)]
    out_shape = [jax.ShapeDtypeStruct((S, len(out_chunks) * PROJ_CHUNK), BF16),
                 jax.ShapeDtypeStruct((S, n_key_lanes), BF16), jax.ShapeDtypeStruct((n_vt, S), BF16)]
    out_specs = [pl.BlockSpec((tm, len(out_chunks) * PROJ_CHUNK), row), pl.BlockSpec((tm, n_key_lanes), row),
                 pl.BlockSpec((n_vt, tm), lambda i: (0, i))]
    scratch = []
    if gates is not None:
        w_f, b_f = gates
        H = w_f.shape[1]
        assert H * LANES == n_key_lanes and H <= LANES
        place = np.zeros((3, LANES, n_key_lanes), np.float32)
        for i in range(3):
            place[i, np.arange(H), np.arange(H) * LANES + HEAD_DIM + i] = 1.0
        in_specs += [_resident((D, LANES)), _resident((1, LANES)), _resident(place.shape)]
        args += [jnp.pad(w_f, ((0, 0), (0, LANES - H))), jnp.pad(b_f, (0, LANES - H)).reshape(1, LANES),
                 jnp.asarray(place, BF16)]
        out_shape += [jax.ShapeDtypeStruct((S, LANES), F32)]
        out_specs += [pl.BlockSpec((tm, LANES), row)]
        scratch = [pltpu.VMEM((1, LANES), F32)]
    return pl.pallas_call(
        functools.partial(_proj_kernel, chunk_scale=tuple(chunk_scale), rope_chunks=tuple(rope_chunks),
                          out_chunks=tuple(out_chunks), key_chunks=tuple(key_chunks),
                          gate_keys=gates is not None, n_vt=n_vt),
        grid=(S // tm,), in_specs=in_specs, out_specs=out_specs, out_shape=out_shape,
        scratch_shapes=scratch, compiler_params=_params("arbitrary"),
        name="proj_gates" if gates is not None else "proj_rope",
    )(*args)


def _head_lanes(shape, a):
    lane = lax.broadcasted_iota(jnp.int32, shape, len(shape) - 1)
    return (lane >= a * HEAD_DIM) & (lane < (a + 1) * HEAD_DIM)


def _pair_queries(q_ref, scale):
    q2 = q_ref[...] * jnp.asarray(scale, q_ref.dtype)
    return [jnp.where(_head_lanes(q2.shape, a), q2, jnp.zeros_like(q2)) for a in range(2)]


def _kv_block(k_ref, v_ref, j, t):
    rows = pl.ds(pl.multiple_of(j * t, t), t)
    return k_ref[rows, :], v_ref[rows, :]


def _sb_kernel(q_ref, k_ref, v_ref, o_ref):
    t = q_ref.shape[0]
    qi = pl.program_id(1)
    scale = HEAD_DIM ** -0.5
    r = lax.broadcasted_iota(jnp.int32, (t, t), 0)
    c = lax.broadcasted_iota(jnp.int32, (t, t), 1)
    strict = c < r
    later = (r > c).astype(BF16)
    qs = _pair_queries(q_ref, scale)
    pair = range(2)

    def inner(k, diagonal):
        z = [_dot_nt(qs[a], k) for a in pair]
        lb = [_log_sigmoid(z[a]) for a in pair]
        l1 = [lb[a] - z[a] for a in pair]
        if diagonal:
            l1 = [jnp.where(strict, l1[a], 0.0) for a in pair]
        parts = [_split2(l1[a]) for a in pair]
        tail = [_dot(parts[a][0], later) + _dot(parts[a][1], later) for a in pair]
        u = [lb[a] + tail[a] for a in pair]
        if diagonal:
            u = [jnp.where(strict, u[a], NEG_INF) for a in pair]
        return u, [tail[a][:, 0:1] + l1[a][:, 0:1] for a in pair]

    def weigh(u, carry, v):
        return [_dot(jnp.exp(u[a] + carry[a]).astype(BF16), v) for a in pair]

    k_own, v_own = _kv_block(k_ref, v_ref, qi, t)
    k_prev, v_prev = _kv_block(k_ref, v_ref, jnp.maximum(qi - 1, 0), t)
    v_prev = v_prev * jnp.where(qi > 0, 1.0, 0.0).astype(v_prev.dtype)
    u_own, sum_own = inner(k_own, True)
    u_prev, sum_prev = inner(k_prev, False)
    acc = weigh(u_own, [0.0, 0.0], v_own)
    pv = weigh(u_prev, sum_own, v_prev)
    acc = [acc[a] + pv[a] for a in pair]
    carry = [sum_own[a] + sum_prev[a] for a in pair]

    def cond(st):
        kb, carry0, carry1, _, _ = st
        return jnp.logical_and(kb >= 0, jnp.max(jnp.maximum(carry0, carry1)) > EXP_ZERO)

    def body(st):
        kb, carry0, carry1, acc0, acc1 = st
        k, v = _kv_block(k_ref, v_ref, kb, t)
        u, sums = inner(k, False)
        pv = weigh(u, [carry0, carry1], v)
        return kb - 1, carry0 + sums[0], carry1 + sums[1], acc0 + pv[0], acc1 + pv[1]

    _, _, _, acc0, acc1 = lax.while_loop(cond, body, (qi - 2, carry[0], carry[1], acc[0], acc[1]))
    o_ref[...] = jnp.where(_head_lanes((t, LANES), 0), acc0, acc1).astype(o_ref.dtype)


class _Flash:
    def __init__(self, qs, s, cm, m, l, acc):
        self.qs, self.s, self.cm, self.m, self.l, self.acc = qs, s, cm, m, l, acc

    def init(self):
        for a in range(2):
            self.m[a] = jnp.full(self.m.shape[1:], KNOCKOUT, F32)
            self.l[a] = jnp.zeros(self.l.shape[1:], F32)
            self.acc[a] = jnp.zeros(self.acc.shape[1:], F32)

    def logits(self, buf, q_slot0, k_ref, rows, mask=None):
        for a in range(2):
            s = _dot_nt(k_ref[rows, a * LANES:(a + 1) * LANES], self.qs[q_slot0 + a])
            if mask is not None:
                s = jnp.where(mask, s, NEG_INF)
            self.s[2 * buf + a] = s
            kt, t = s.shape
            top = jnp.max(jnp.max(s.reshape(kt // 8, 8, t), axis=0), axis=0, keepdims=True)
            self.cm[2 * buf + a] = jnp.broadcast_to(top, (8, t))

    def accumulate(self, buf, vt_ref, rows):
        for a in range(2):
            m_new = jnp.maximum(self.m[a], self.cm[2 * buf + a])
            alpha = jnp.exp2(self.m[a] - m_new)
            s = self.s[2 * buf + a]
            kt, t = s.shape
            p = jnp.exp2(s - m_new[0:1, :])
            self.l[a] = alpha * self.l[a] + jnp.sum(p.reshape(kt // 8, 8, t), axis=0)
            pv = _dot(vt_ref[a * HEAD_DIM:(a + 1) * HEAD_DIM, rows], p.astype(BF16))
            self.acc[a] = alpha[0:1, :] * self.acc[a] + pv
            self.m[a] = m_new

    def result(self):
        out = [self.acc[a] / jnp.sum(self.l[a], axis=0, keepdims=True) for a in range(2)]
        return jnp.concatenate(out, axis=0).T


def _flash_scratch(t, kt):
    return [pltpu.VMEM((4, t, LANES), BF16), pltpu.VMEM((4, kt, t), F32), pltpu.VMEM((4, 8, t), F32),
            pltpu.VMEM((2, 8, t), F32), pltpu.VMEM((2, 8, t), F32), pltpu.VMEM((2, HEAD_DIM, t), F32)]


def _split_heads(q_ref):
    q2 = q_ref[...]
    low = lax.broadcasted_iota(jnp.int32, q2.shape, 1) < HEAD_DIM
    moved = pltpu.roll(q2.astype(F32), HEAD_DIM, 1).astype(q2.dtype)
    return [jnp.where(low, q, jnp.zeros_like(q2)) for q in (q2, moved)]


def _diagonal_mask(kt, t, shift):
    key = lax.broadcasted_iota(jnp.int32, (kt, t), 0)
    query = lax.broadcasted_iota(jnp.int32, (kt, t), 1)
    return key <= query + shift


def _moba_kernel(q_ref, k_ref, vt_ref, o_ref, km32, km_hi, km_mid, km_lo, *flash_refs):
    t = q_ref.shape[0]
    nb = k_ref.shape[0] // MOBA_BLOCK
    qi = pl.program_id(1)
    fl = _Flash(*flash_refs)
    kt = fl.s.shape[1]
    n_tiles = k_ref.shape[0] // kt
    n_past = (qi * t) // kt

    @pl.when(qi == 0)
    def _():
        km32[...] = jnp.zeros_like(km32)

        def centroid(b, _):
            blk = k_ref[pl.ds(pl.multiple_of(b * MOBA_BLOCK, MOBA_BLOCK), MOBA_BLOCK), :].astype(F32)
            km32[pl.ds(HEAD_DIM + b, 1), :] = jnp.sum(blk, axis=0, keepdims=True) * (1.0 / MOBA_BLOCK)
            return 0
        lax.fori_loop(0, nb, centroid, 0)
        hi, mid, lo = _split3(km32[...])
        km_hi[...] = hi
        km_mid[...] = mid
        km_lo[...] = lo

    lane = lax.broadcasted_iota(jnp.int32, (t, LANES), 1)
    low = lane < HEAD_DIM
    blk = lane - HEAD_DIM
    lane_f = lane.astype(F32)
    for a, q in enumerate(_split_heads(q_ref)):
        cols = slice(a * LANES, (a + 1) * LANES)
        gate = _dot_nt(q, km_hi[:, cols]) + _dot_nt(q, km_mid[:, cols]) + _dot_nt(q, km_lo[:, cols])
        g = jnp.where(jnp.logical_and(blk >= 0, blk < qi), gate, NEG_INF)
        sel = jnp.zeros((t, LANES), F32)
        for rank in range(MOBA_TOPK):
            best = jnp.max(g, axis=1, keepdims=True)
            idx = jnp.min(jnp.where(g == best, lane_f, float(LANES)), axis=1, keepdims=True)
            pick = lane_f == idx
            sel = jnp.where(pick, jnp.where(rank < qi, 1.0, 0.0), sel)
            g = jnp.where(pick, KNOCKOUT, g)
        past_bias = jnp.where(sel > 0.0, 0.0, NEG_INF).astype(q.dtype)
        own_bias = jnp.where(jnp.logical_or(sel > 0.0, blk == qi), 0.0, NEG_INF).astype(q.dtype)
        fl.qs[a] = jnp.where(low, q, past_bias)
        fl.qs[2 + a] = jnp.where(low, q, own_bias)

    def rows(j):
        return pl.ds(pl.multiple_of(jnp.minimum(j, n_tiles - 1) * kt, kt), kt)

    fl.init()
    fl.logits(0, 2, k_ref, rows(n_past), _diagonal_mask(kt, t, qi * t - n_past * kt))
    fl.logits(1, 0, k_ref, rows(0))
    fl.accumulate(0, vt_ref, rows(n_past))

    def body(i, _):
        fl.logits(0, 0, k_ref, rows(2 * i + 1))
        fl.accumulate(1, vt_ref, rows(2 * i))
        fl.logits(1, 0, k_ref, rows(2 * i + 2))
        fl.accumulate(0, vt_ref, rows(2 * i + 1))
        return 0

    lax.fori_loop(0, n_past // 2, body, 0)

    @pl.when(n_past % 2 == 1)
    def _():
        fl.accumulate(1, vt_ref, rows(n_past - 1))

    o_ref[...] = fl.result().astype(o_ref.dtype)


def _fox_kernel(cend_ref, q_ref, k_ref, vt_ref, o_ref, knorm_ref, *flash_refs):
    t = q_ref.shape[0]
    hp = pl.program_id(0)
    qi = pl.program_id(1)
    fl = _Flash(*flash_refs)
    kt = fl.s.shape[1]
    n_tiles = k_ref.shape[0] // kt
    n_past = (qi * t) // kt

    @pl.when(qi == 0)
    def _():
        key_lanes = lax.broadcasted_iota(jnp.int32, (1, 2 * LANES), 1) % LANES < HEAD_DIM

        def step(b, best):
            k = k_ref[pl.ds(pl.multiple_of(b * kt, kt), kt), :].astype(F32)
            sq = jnp.where(key_lanes, k * k, 0.0)
            return tuple(jnp.maximum(best[a], jnp.sum(sq[:, a * LANES:(a + 1) * LANES], axis=1, keepdims=True))
                         for a in range(2))
        zero = jnp.zeros((kt, 1), F32)
        best = lax.fori_loop(0, n_tiles, step, (zero, zero))
        for a in range(2):
            knorm_ref[a:a + 1, :] = jnp.broadcast_to(jnp.max(best[a], axis=0, keepdims=True), (1, LANES))

    lane = lax.broadcasted_iota(jnp.int32, (t, LANES), 1)
    minus_one = jnp.where(jnp.logical_and(lane >= HEAD_DIM, lane < HEAD_DIM + 3), -1.0, 0.0).astype(q_ref.dtype)
    heads = _split_heads(q_ref)
    for a in range(2):
        fl.qs[a] = jnp.where(lane < HEAD_DIM, heads[a], minus_one)

    def rows(j):
        return pl.ds(pl.multiple_of(jnp.maximum(j, 0) * kt, kt), kt)

    fl.init()
    fl.logits(0, 0, k_ref, rows(n_past), _diagonal_mask(kt, t, qi * t - n_past * kt))
    fl.logits(1, 0, k_ref, rows(n_past - 1))
    fl.accumulate(0, vt_ref, rows(n_past))

    reach = []
    for a in range(2):
        q32 = heads[a].astype(F32).T
        qn2 = jnp.sum(q32 * q32, axis=0, keepdims=True)
        zmax = jnp.sqrt(qn2 * knorm_ref[a:a + 1, 0:1]) * 1.001
        reach.append(jnp.max(zmax - fl.m[a][0:1, :]))

    def alive(j):
        jj = jnp.maximum(j, 0)
        live = [reach[a] - cend_ref[(2 * hp + a) * n_tiles + jj] > (EXP_ZERO - BOUND_SLACK) * LOG2E
                for a in range(2)]
        return jnp.logical_and(j >= 0, jnp.logical_or(live[0], live[1]))

    def body(j):
        fl.logits(0, 0, k_ref, rows(j - 1))
        fl.accumulate(1, vt_ref, rows(j))
        fl.logits(1, 0, k_ref, rows(j - 2))
        fl.accumulate(0, vt_ref, rows(j - 1))
        return j - 2

    j = lax.while_loop(lambda j: alive(j - 1), body, n_past - 1)

    @pl.when(alive(j))
    def _():
        fl.accumulate(1, vt_ref, rows(j))

    o_ref[...] = fl.result().astype(o_ref.dtype)


def _sb_attention(qkv, q_block0, k_block0, v_block0, n_pairs):
    S = qkv.shape[0]
    t = min(ATT_TILE, S)
    assert S % t == 0
    return pl.pallas_call(
        _sb_kernel, grid=(n_pairs, S // t),
        in_specs=[pl.BlockSpec((t, LANES), lambda p, i: (i, q_block0 + p)),
                  pl.BlockSpec((S, LANES), lambda p, i: (0, k_block0 + p)),
                  pl.BlockSpec((S, LANES), lambda p, i: (0, v_block0 + p))],
        out_specs=pl.BlockSpec((t, LANES), lambda p, i: (i, p)),
        out_shape=jax.ShapeDtypeStruct((S, n_pairs * LANES), BF16),
        compiler_params=_params("arbitrary", "arbitrary"), name="sb",
    )(qkv, qkv, qkv)


def _flash_attention(kind, q_arr, q_block0, keys, values_t, n_pairs, c_end=None):
    S = q_arr.shape[0]
    t = min(ATT_TILE, S)
    kt = min(KEY_TILE[kind], S)
    assert S % kt == 0 and kt % t == 0 and t == MOBA_BLOCK
    in_specs = [pl.BlockSpec((t, LANES), lambda p, i: (i, q_block0 + p)),
                pl.BlockSpec((S, 2 * LANES), lambda p, i: (0, p)),
                pl.BlockSpec((LANES, S), lambda p, i: (p, 0))]
    args = [q_arr, keys, values_t]
    if kind == "moba":
        body = _moba_kernel
        assert S // MOBA_BLOCK <= HEAD_DIM
        scratch = [pltpu.VMEM((LANES, 2 * LANES), F32)] + [pltpu.VMEM((LANES, 2 * LANES), BF16)] * 3
    else:
        body = _fox_kernel
        in_specs = [pl.BlockSpec(memory_space=pltpu.SMEM)] + in_specs
        args = [c_end] + args
        scratch = [pltpu.VMEM((8, LANES), F32)]
    return pl.pallas_call(
        body, grid=(n_pairs, S // t), in_specs=in_specs,
        out_specs=pl.BlockSpec((t, LANES), lambda p, i: (i, p)),
        out_shape=jax.ShapeDtypeStruct((S, n_pairs * LANES), BF16),
        scratch_shapes=scratch + _flash_scratch(t, kt),
        compiler_params=_params("arbitrary", "arbitrary"), name=kind,
    )(*args)


def _memkv_kernel(mem_ref, g_ref, w_ref, o_ref):
    o_ref[...] = _dot(_rms(mem_ref[...], g_ref[...]).astype(BF16), w_ref[...]).astype(o_ref.dtype)


def _memory_kv(mem, g, w_kv):
    M, D = mem.shape
    N = w_kv.shape[1]
    return pl.pallas_call(
        _memkv_kernel, out_shape=jax.ShapeDtypeStruct((M, N), BF16), name="memkv",
        compiler_params=pltpu.CompilerParams(vmem_limit_bytes=VMEM_LIMIT),
    )(mem, g.reshape(1, D), w_kv.astype(BF16))


def _post_kernel(*refs, n_att):
    att_refs = refs[:n_att]
    h_ref, wo_ref, g_ref, wq_ref, kv_ref, wxo_ref, o_ref = refs[n_att:]
    tm = h_ref.shape[0]
    xa_dim = wq_ref.shape[1]
    hd = xa_dim // XA_HEADS
    h1 = h_ref[...]
    col = 0
    for a_ref in att_refs:
        w = a_ref.shape[1]
        h1 = h1 + _dot(a_ref[...], wo_ref[col:col + w, :])
        col += w
    q = _dot(_rms(h1, g_ref[...]).astype(BF16), wq_ref[...]).astype(BF16)
    kv = kv_ref[...]
    k = kv[:, :xa_dim]
    v = kv[:, xa_dim:]
    lane = lax.broadcasted_iota(jnp.int32, (tm, xa_dim), 1)
    o = jnp.zeros((tm, xa_dim), F32)
    for hh in range(XA_HEADS):
        mine = (lane >= hh * hd) & (lane < (hh + 1) * hd)
        s = _dot_nt(jnp.where(mine, q, jnp.zeros_like(q)), k) * (hd ** -0.5)
        m = jnp.max(s, axis=1, keepdims=True)
        p = jnp.exp(s - m)
        l = jnp.sum(p, axis=1, keepdims=True)
        o = jnp.where(mine, _dot(p.astype(BF16), v) / l, o)
    o_ref[...] = h1 + _dot(o.astype(BF16), wxo_ref[...])


def _post(att_parts, h, w_out, g_xa, w_q, kv, w_xo):
    S, D = h.shape
    tm = min(ROW_TILE, S)
    row = lambda i: (i, 0)
    in_specs = [pl.BlockSpec((tm, a.shape[1]), row) for a in att_parts]
    in_specs += [pl.BlockSpec((tm, D), row), _resident(w_out.shape), _resident((1, D)),
                 _resident(w_q.shape), _resident(kv.shape), _resident(w_xo.shape)]
    return pl.pallas_call(
        functools.partial(_post_kernel, n_att=len(att_parts)),
        grid=(S // tm,), in_specs=in_specs, out_specs=pl.BlockSpec((tm, D), row),
        out_shape=jax.ShapeDtypeStruct((S, D), F32), compiler_params=_params("arbitrary"), name="post",
    )(*att_parts, h, w_out.astype(BF16), g_xa.reshape(1, D), w_q.astype(BF16), kv, w_xo.astype(BF16))


def _ffn_kernel(*refs, d_ff, final):
    if final:
        h_ref, g_ref, wup_ref, cw_ref, cb_ref, wdn_ref, gfin_ref, o_ref, prev_ref = refs
    else:
        h_ref, g_ref, wup_ref, cw_ref, cb_ref, wdn_ref, o_ref, prev_ref = refs
    tm = h_ref.shape[0]

    @pl.when(pl.program_id(0) == 0)
    def _():
        prev_ref[...] = jnp.zeros_like(prev_ref)

    h = h_ref[...]
    hn = _rms(h, g_ref[...]).astype(BF16)
    rowi = lax.broadcasted_iota(jnp.int32, (tm, FF_CHUNK), 0)

    def up(c):
        return [_dot(hn, wup_ref[:, col0:col0 + FF_CHUNK]) for col0 in (c * FF_CHUNK, d_ff + c * FF_CHUNK)]

    def conv(col0, u):
        cols = slice(col0, col0 + FF_CHUNK)
        prev = prev_ref[:, cols]
        p1, p2 = prev[7:8, :], prev[6:7, :]
        u1 = jnp.where(rowi == 0, p1, pltpu.roll(u, 1, 0))
        u2 = jnp.where(rowi == 0, p2, jnp.where(rowi == 1, p1, pltpu.roll(u, 2, 0)))
        prev_ref[:, cols] = u[tm - 8:tm, :]
        w = cw_ref[:, cols]
        return cb_ref[:, cols] + u2 * w[0:1, :] + u1 * w[1:2, :] + u * w[2:3, :]

    acc = h
    n_chunks = d_ff // FF_CHUNK
    u_next = up(0)
    for c in range(n_chunks):
        u_gate, u_val = u_next
        if c + 1 < n_chunks:
            u_next = up(c + 1)
        gate = conv(c * FF_CHUNK, u_gate)
        val = conv(d_ff + c * FF_CHUNK, u_val)
        act = gate * (1.0 / (1.0 + jnp.exp(-gate))) * val
        acc = acc + _dot(act.astype(BF16), wdn_ref[c * FF_CHUNK:(c + 1) * FF_CHUNK, :])
    if final:
        acc = _rms(acc, gfin_ref[...])
    o_ref[...] = acc


def _ffn(h, g, w_up, conv_w, conv_b, w_down, g_final=None):
    S, D = h.shape
    d_ff = w_down.shape[0]
    tm = min(ROW_TILE, S)
    assert d_ff % FF_CHUNK == 0 and conv_w.shape[0] == CONV_WIDTH
    row = lambda i: (i, 0)
    in_specs = [pl.BlockSpec((tm, D), row), _resident((1, D)), _resident(w_up.shape),
                _resident(conv_w.shape), _resident((1, 2 * d_ff)), _resident(w_down.shape)]
    args = [h, g.reshape(1, D), w_up.astype(BF16), conv_w, conv_b.reshape(1, 2 * d_ff), w_down.astype(BF16)]
    if g_final is not None:
        in_specs.append(_resident((1, D)))
        args.append(g_final.reshape(1, D))
    return pl.pallas_call(
        functools.partial(_ffn_kernel, d_ff=d_ff, final=g_final is not None),
        grid=(S // tm,), in_specs=in_specs, out_specs=pl.BlockSpec((tm, D), row),
        out_shape=jax.ShapeDtypeStruct((S, D), F32),
        scratch_shapes=[pltpu.VMEM((8, 2 * d_ff), F32)],
        compiler_params=_params("arbitrary"), name="ffn",
    )(*args)


def kernel(x, mem, positions, norm_mix_g, norm_xa_g, norm_mem_g, norm_ffn_g, ab_w_in, ab_w_out, fox_w_in, fox_b_f, fox_w_out, xa_w_q, xa_w_kv, xa_w_out, ffn_w_up, ffn_conv_w, ffn_conv_b, ffn_w_down, final_norm_g):
    B, S, D = x.shape
    depth = norm_mix_g.shape[0]
    n_pairs = D // LANES
    q_scale2 = HEAD_DIM ** -0.5 * LOG2E
    outs = []
    for b in range(B):
        h = x[b]
        for layer in range(depth):
            n = layer // 2
            if layer % 2 == 0:
                assert D == 2 * PROJ_CHUNK
                half = n_pairs // 2
                w_in = ab_w_in[n]
                qkv, k_moba, vt_moba = _project(
                    h, norm_mix_g[layer], w_in, w_in[:, 2 * D + PROJ_CHUNK:].T,
                    (1.0, q_scale2, 1.0, 1.0, 1.0, 1.0), range(6), (3,), rope=(positions[b], (1, 3)))
                att = [_sb_attention(qkv, 0, n_pairs, 2 * n_pairs, half),
                       _flash_attention("moba", qkv, half, k_moba, vt_moba, half)]
                w_out = ab_w_out[n]
            else:
                w_in = fox_w_in[n]
                q, keys, vt, c2 = _project(
                    h, norm_mix_g[layer], w_in[:, :2 * D], w_in[:, 2 * D:3 * D].T,
                    (q_scale2, q_scale2, 1.0, 1.0), (0, 1), (2, 3), gates=(w_in[:, 3 * D:], fox_b_f[n]))
                kt = min(KEY_TILE["fox"], S)
                c_end = c2[kt - 1::kt, :fox_b_f.shape[1]].T.reshape(-1)
                att = [_flash_attention("fox", q, 0, keys, vt, n_pairs, c_end)]
                w_out = fox_w_out[n]
            kv = _memory_kv(mem[b], norm_mem_g[layer], xa_w_kv[layer])
            h = _post(att, h, w_out, norm_xa_g[layer], xa_w_q[layer], kv, xa_w_out[layer])
            h = _ffn(h, norm_ffn_g[layer], ffn_w_up[layer], ffn_conv_w[layer], ffn_conv_b[layer],
                     ffn_w_down[layer], final_norm_g if layer == depth - 1 else None)
        outs.append(h)
    return jnp.stack(outs)
```

```python
import functools

import numpy as np
import jax
import jax.numpy as jnp
from jax import lax
from jax.experimental import pallas as pl
from jax.experimental.pallas import tpu as pltpu

F32 = jnp.float32
BF16 = jnp.bfloat16

HEAD_DIM = 64
LANES = 128
MOBA_BLOCK = 256
MOBA_TOPK = 3
ROPE_THETA = 10000.0
XA_HEADS = 4
CONV_WIDTH = 3
RMS_EPS = 1e-6
NEG_INF = -1e9
KNOCKOUT = -3.0e38
LOG2E = float(np.log2(np.e))
EXP_ZERO = -104.0
BOUND_SLACK = 1.0
ATT_TILE = 256
FLASH_TILE = {"moba": 1024, "fox": 512}
KEY_TILE = {"moba": 1024, "fox": 512}
ROW_TILE = 512
PROJ_CHUNK = 512
FF_CHUNK = 256
VMEM_LIMIT = 56 * 1024 * 1024

_NT = (((1,), (1,)), ((), ()))


def _dot(a, b):
    return jnp.dot(a, b, preferred_element_type=F32)


def _dot_nt(a, b):
    return lax.dot_general(a, b, _NT, preferred_element_type=F32)


def _split2(x):
    hi = x.astype(BF16)
    lo = (x - hi.astype(F32)).astype(BF16)
    return hi, lo


def _split3(x):
    hi = x.astype(BF16)
    r = x - hi.astype(F32)
    mid = r.astype(BF16)
    lo = (r - mid.astype(F32)).astype(BF16)
    return hi, mid, lo


def _rms(x, g):
    return x * lax.rsqrt(jnp.mean(x * x, axis=-1, keepdims=True) + RMS_EPS) * g


def _log_sigmoid(z):
    return jnp.minimum(z, 0.0) - jnp.log(1.0 + jnp.exp(-jnp.abs(z)))


def _resident(shape):
    return pl.BlockSpec(shape, lambda *_: (0,) * len(shape), pipeline_mode=pl.Buffered(1))


def _params(*sem):
    return pltpu.CompilerParams(dimension_semantics=sem, vmem_limit_bytes=VMEM_LIMIT)


def _proj_kernel(*refs, chunk_scale, rope_chunks, out_chunks, key_chunks, gate_keys, n_vt):
    refs = list(refs)
    x_ref, g_ref, w_ref = refs[:3]
    del refs[:3]
    pos_ref = refs.pop(0) if rope_chunks else None
    wvt_ref = refs.pop(0)
    if gate_keys:
        wf_ref, bf_ref, place_ref = refs[:3]
        del refs[:3]
    o_ref, kaug_ref, vt_ref = refs[:3]
    del refs[:3]
    if gate_keys:
        ccol_ref, carry_ref = refs
    tm = x_ref.shape[0]
    hn32 = _rms(x_ref[...], g_ref[...])
    hn = hn32.astype(BF16)
    lane = lax.broadcasted_iota(jnp.int32, (tm, LANES), 1)
    low = lane < HEAD_DIM

    if rope_chunks:
        half = HEAD_DIM // 2
        first_half = (lane % HEAD_DIM) < half
        freq_id = (lax.broadcasted_iota(jnp.int32, (1, LANES), 1) % half).astype(F32)
        inv_freq = jnp.power(ROPE_THETA, -freq_id / half)
        ang = pos_ref[...].astype(F32) * inv_freq
        cos, sin = jnp.cos(ang), jnp.sin(ang)
        sin_signed = jnp.where(first_half, -sin, sin)

    if gate_keys:
        @pl.when(pl.program_id(0) == 0)
        def _():
            carry_ref[...] = jnp.zeros_like(carry_ref)

        h_hi, h_lo = _split2(hn32)
        w_hi, w_lo = _split2(wf_ref[...])
        f = _dot(h_hi, w_hi) + _dot(h_lo, w_hi) + _dot(h_hi, w_lo) + bf_ref[...]
        r = lax.broadcasted_iota(jnp.int32, (tm, tm), 0)
        cidx = lax.broadcasted_iota(jnp.int32, (tm, tm), 1)
        incl = (cidx <= r).astype(BF16)
        c_col = carry_ref[...]
        for piece in _split3(_log_sigmoid(f)):
            c_col = c_col + _dot(incl, piece)
        carry_ref[...] = c_col[tm - 1:tm, :]
        c2 = c_col * LOG2E
        ccol_ref[...] = c2
        extras = sum(_dot(piece, place_ref[i]) for i, piece in enumerate(_split3(c2)))
    elif key_chunks:
        row = pl.program_id(0) * tm + lax.broadcasted_iota(jnp.int32, (tm, LANES), 0)
        one_hot = (lane - HEAD_DIM == row // MOBA_BLOCK).astype(F32)

    for c in range(len(chunk_scale)):
        y = _dot(hn, w_ref[:, c * PROJ_CHUNK:(c + 1) * PROJ_CHUNK])
        parts = [y[:, s * LANES:(s + 1) * LANES] for s in range(PROJ_CHUNK // LANES)]
        if c in rope_chunks:
            parts = [ys * cos + sin_signed * jnp.where(first_half, pltpu.roll(ys, LANES - HEAD_DIM // 2, 1),
                                                       pltpu.roll(ys, HEAD_DIM // 2, 1)) for ys in parts]
        if c in key_chunks:
            base = key_chunks.index(c) * 2 * len(parts)
            for s, ys in enumerate(parts):
                for a, head in enumerate((ys, pltpu.roll(ys, HEAD_DIM, 1))):
                    col = (base + 2 * s + a) * LANES
                    upper = extras[:, col:col + LANES] if gate_keys else one_hot
                    kaug_ref[:, col:col + LANES] = jnp.where(low, head, upper).astype(BF16)
        if c in out_chunks:
            y = jnp.concatenate(parts, axis=1)
            if chunk_scale[c] != 1.0:
                y = y * chunk_scale[c]
            col = out_chunks.index(c) * PROJ_CHUNK
            o_ref[:, col:col + PROJ_CHUNK] = y.astype(o_ref.dtype)

    for i in range(n_vt // PROJ_CHUNK):
        rows = slice(i * PROJ_CHUNK, (i + 1) * PROJ_CHUNK)
        vt_ref[rows, :] = _dot_nt(wvt_ref[rows, :], hn).astype(BF16)


def _project(x, g, w, w_vt, chunk_scale, out_chunks, key_chunks, *, rope=None, gates=None):
    S, D = x.shape
    tm = min(ROW_TILE, S)
    n_vt = w_vt.shape[0]
    n_key_lanes = len(key_chunks) * 2 * PROJ_CHUNK
    assert S % tm == 0 and w.shape[1] == PROJ_CHUNK * len(chunk_scale) and n_vt % PROJ_CHUNK == 0
    row = lambda i: (i, 0)
    in_specs = [pl.BlockSpec((tm, D), row), _resident((1, D)), _resident(w.shape)]
    args = [x, g.reshape(1, D), w.astype(BF16)]
    rope_chunks = ()
    if rope is not None:
        positions, rope_chunks = rope
        assert S // MOBA_BLOCK <= HEAD_DIM
        in_specs += [pl.BlockSpec((tm, 1), row)]
        args += [positions.reshape(S, 1)]
    in_specs += [_resident(w_vt.shape)]
    args += [w_vt.astype(BF16)]
    out_shape = [jax.ShapeDtypeStruct((S, len(out_chunks) * PROJ_CHUNK), BF16),
                 jax.ShapeDtypeStruct((S, n_key_lanes), BF16), jax.ShapeDtypeStruct((n_vt, S), BF16)]
    out_specs = [pl.BlockSpec((tm, len(out_chunks) * PROJ_CHUNK), row), pl.BlockSpec((tm, n_key_lanes), row),
                 pl.BlockSpec((n_vt, tm), lambda i: (0, i))]
    scratch = []
    if gates is not None:
        w_f, b_f = gates
        H = w_f.shape[1]
        assert H * LANES == n_key_lanes and H <= LANES
        place = np.zeros((3, LANES, n_key_lanes), np.float32)
        for i in range(3):
            place[i, np.arange(H), np.arange(H) * LANES + HEAD_DIM + i] = 1.0
        in_specs += [_resident((D, LANES)), _resident((1, LANES)), _resident(place.shape)]
        args += [jnp.pad(w_f, ((0, 0), (0, LANES - H))), jnp.pad(b_f, (0, LANES - H)).reshape(1, LANES),
                 jnp.asarray(place, BF16)]
        out_shape += [jax.ShapeDtypeStruct((S, LANES), F32)]
        out_specs += [pl.BlockSpec((tm, LANES), row)]
        scratch = [pltpu.VMEM((1, LANES), F32)]
    return pl.pallas_call(
        functools.partial(_proj_kernel, chunk_scale=tuple(chunk_scale), rope_chunks=tuple(rope_chunks),
                          out_chunks=tuple(out_chunks), key_chunks=tuple(key_chunks),
                          gate_keys=gates is not None, n_vt=n_vt),
        grid=(S // tm,), in_specs=in_specs, out_specs=out_specs, out_shape=out_shape,
        scratch_shapes=scratch, compiler_params=_params("arbitrary"),
        name="proj_gates" if gates is not None else "proj_rope",
    )(*args)


def _head_lanes(shape, a):
    lane = lax.broadcasted_iota(jnp.int32, shape, len(shape) - 1)
    return (lane >= a * HEAD_DIM) & (lane < (a + 1) * HEAD_DIM)


def _pair_queries(q_ref, scale):
    q2 = q_ref[...] * jnp.asarray(scale, q_ref.dtype)
    return [jnp.where(_head_lanes(q2.shape, a), q2, jnp.zeros_like(q2)) for a in range(2)]


def _kv_block(k_ref, v_ref, j, t):
    rows = pl.ds(pl.multiple_of(j * t, t), t)
    return k_ref[rows, :], v_ref[rows, :]


def _sb_kernel(q_ref, k_ref, v_ref, o_ref):
    t = q_ref.shape[0]
    qi = pl.program_id(1)
    scale = HEAD_DIM ** -0.5
    r = lax.broadcasted_iota(jnp.int32, (t, t), 0)
    c = lax.broadcasted_iota(jnp.int32, (t, t), 1)
    strict = c < r
    later = (r > c).astype(BF16)
    qs = _pair_queries(q_ref, scale)
    pair = range(2)

    def inner(k, diagonal):
        z = [_dot_nt(qs[a], k) for a in pair]
        lb = [_log_sigmoid(z[a]) for a in pair]
        l1 = [lb[a] - z[a] for a in pair]
        if diagonal:
            l1 = [jnp.where(strict, l1[a], 0.0) for a in pair]
        parts = [_split2(l1[a]) for a in pair]
        tail = [_dot(parts[a][0], later) + _dot(parts[a][1], later) for a in pair]
        u = [lb[a] + tail[a] for a in pair]
        if diagonal:
            u = [jnp.where(strict, u[a], NEG_INF) for a in pair]
        return u, [tail[a][:, 0:1] + l1[a][:, 0:1] for a in pair]

    def weigh(u, carry, v):
        return [_dot(jnp.exp(u[a] + carry[a]).astype(BF16), v) for a in pair]

    k_own, v_own = _kv_block(k_ref, v_ref, qi, t)
    k_prev, v_prev = _kv_block(k_ref, v_ref, jnp.maximum(qi - 1, 0), t)
    v_prev = v_prev * jnp.where(qi > 0, 1.0, 0.0).astype(v_prev.dtype)
    u_own, sum_own = inner(k_own, True)
    u_prev, sum_prev = inner(k_prev, False)
    acc = weigh(u_own, [0.0, 0.0], v_own)
    pv = weigh(u_prev, sum_own, v_prev)
    acc = [acc[a] + pv[a] for a in pair]
    carry = [sum_own[a] + sum_prev[a] for a in pair]

    def cond(st):
        kb, carry0, carry1, _, _ = st
        return jnp.logical_and(kb >= 0, jnp.max(jnp.maximum(carry0, carry1)) > EXP_ZERO)

    def body(st):
        kb, carry0, carry1, acc0, acc1 = st
        k, v = _kv_block(k_ref, v_ref, kb, t)
        u, sums = inner(k, False)
        pv = weigh(u, [carry0, carry1], v)
        return kb - 1, carry0 + sums[0], carry1 + sums[1], acc0 + pv[0], acc1 + pv[1]

    _, _, _, acc0, acc1 = lax.while_loop(cond, body, (qi - 2, carry[0], carry[1], acc[0], acc[1]))
    o_ref[...] = jnp.where(_head_lanes((t, LANES), 0), acc0, acc1).astype(o_ref.dtype)


class _Flash:
    def __init__(self, qs, s, cm, m, l, acc):
        self.qs, self.s, self.cm, self.m, self.l, self.acc = qs, s, cm, m, l, acc

    def init(self):
        for a in range(2):
            self.m[a] = jnp.full(self.m.shape[1:], KNOCKOUT, F32)
            self.l[a] = jnp.zeros(self.l.shape[1:], F32)
            self.acc[a] = jnp.zeros(self.acc.shape[1:], F32)

    def logits(self, buf, q_slot0, k_ref, rows, mask=None):
        for a in range(2):
            s = _dot_nt(k_ref[rows, a * LANES:(a + 1) * LANES], self.qs[q_slot0 + a])
            if mask is not None:
                s = jnp.where(mask, s, NEG_INF)
            self.s[2 * buf + a] = s
            kt, t = s.shape
            top = jnp.max(jnp.max(s.reshape(kt // 8, 8, t), axis=0), axis=0, keepdims=True)
            self.cm[2 * buf + a] = jnp.broadcast_to(top, (8, t))

    def accumulate(self, buf, vt_ref, rows):
        for a in range(2):
            m_new = jnp.maximum(self.m[a], self.cm[2 * buf + a])
            alpha = jnp.exp2(self.m[a] - m_new)
            s = self.s[2 * buf + a]
            kt, t = s.shape
            p = jnp.exp2(s - m_new[0:1, :])
            self.l[a] = alpha * self.l[a] + jnp.sum(p.reshape(kt // 8, 8, t), axis=0)
            pv = _dot(vt_ref[a * HEAD_DIM:(a + 1) * HEAD_DIM, rows], p.astype(BF16))
            self.acc[a] = alpha[0:1, :] * self.acc[a] + pv
            self.m[a] = m_new

    def result(self):
        out = [self.acc[a] / jnp.sum(self.l[a], axis=0, keepdims=True) for a in range(2)]
        return jnp.concatenate(out, axis=0).T


def _flash_scratch(t, kt):
    return [pltpu.VMEM((4, t, LANES), BF16), pltpu.VMEM((4, kt, t), F32), pltpu.VMEM((4, 8, t), F32),
            pltpu.VMEM((2, 8, t), F32), pltpu.VMEM((2, 8, t), F32), pltpu.VMEM((2, HEAD_DIM, t), F32)]


def _split_heads(q_ref):
    q2 = q_ref[...]
    low = lax.broadcasted_iota(jnp.int32, q2.shape, 1) < HEAD_DIM
    moved = pltpu.roll(q2.astype(F32), HEAD_DIM, 1).astype(q2.dtype)
    return [jnp.where(low, q, jnp.zeros_like(q2)) for q in (q2, moved)]


def _diagonal_mask(kt, t, shift):
    key = lax.broadcasted_iota(jnp.int32, (kt, t), 0)
    query = lax.broadcasted_iota(jnp.int32, (kt, t), 1)
    return key <= query + shift


def _moba_kernel(q_ref, k_ref, vt_ref, o_ref, km32, km_hi, km_mid, km_lo, *flash_refs):
    t = q_ref.shape[0]
    nb = k_ref.shape[0] // MOBA_BLOCK
    qi = pl.program_id(1)
    fl = _Flash(*flash_refs)
    kt = fl.s.shape[1]
    n_tiles = k_ref.shape[0] // kt
    n_past = (qi * t) // kt

    @pl.when(qi == 0)
    def _():
        km32[...] = jnp.zeros_like(km32)

        def centroid(b, _):
            blk = k_ref[pl.ds(pl.multiple_of(b * MOBA_BLOCK, MOBA_BLOCK), MOBA_BLOCK), :].astype(F32)
            km32[pl.ds(HEAD_DIM + b, 1), :] = jnp.sum(blk, axis=0, keepdims=True) * (1.0 / MOBA_BLOCK)
            return 0
        lax.fori_loop(0, nb, centroid, 0)
        hi, mid, lo = _split3(km32[...])
        km_hi[...] = hi
        km_mid[...] = mid
        km_lo[...] = lo

    lane = lax.broadcasted_iota(jnp.int32, (t, LANES), 1)
    low = lane < HEAD_DIM
    blk = lane - HEAD_DIM
    own = (qi * t + lax.broadcasted_iota(jnp.int32, (t, LANES), 0)) // MOBA_BLOCK
    slot = lax.broadcasted_iota(jnp.int32, (LANES, t), 0)
    slot_f = slot.astype(F32)
    own_t = (qi * t + lax.broadcasted_iota(jnp.int32, (LANES, t), 1)) // MOBA_BLOCK
    candidate = jnp.logical_and(slot >= HEAD_DIM, slot - HEAD_DIM < own_t)
    for a, q in enumerate(_split_heads(q_ref)):
        cols = slice(a * LANES, (a + 1) * LANES)
        gate = _dot_nt(km_hi[:, cols], q) + _dot_nt(km_mid[:, cols], q) + _dot_nt(km_lo[:, cols], q)
        g = jnp.where(candidate, gate, NEG_INF)
        sel_t = jnp.zeros((LANES, t), F32)
        for rank in range(MOBA_TOPK):
            best = jnp.max(g, axis=0, keepdims=True)
            idx = jnp.min(jnp.where(g == best, slot_f, float(LANES)), axis=0, keepdims=True)
            pick = slot_f == idx
            sel_t = jnp.where(pick, jnp.where(rank < own_t, 1.0, 0.0), sel_t)
            g = jnp.where(pick, KNOCKOUT, g)
        sel = sel_t.T
        past_bias = jnp.where(sel > 0.0, 0.0, NEG_INF).astype(q.dtype)
        own_bias = jnp.where(jnp.logical_or(sel > 0.0, blk == own), 0.0, NEG_INF).astype(q.dtype)
        fl.qs[a] = jnp.where(low, q, past_bias)
        fl.qs[2 + a] = jnp.where(low, q, own_bias)

    def rows(j):
        return pl.ds(pl.multiple_of(jnp.minimum(j, n_tiles - 1) * kt, kt), kt)

    fl.init()
    fl.logits(0, 2, k_ref, rows(n_past), _diagonal_mask(kt, t, qi * t - n_past * kt))
    fl.logits(1, 0, k_ref, rows(0))
    fl.accumulate(0, vt_ref, rows(n_past))

    def body(i, _):
        fl.logits(0, 0, k_ref, rows(2 * i + 1))
        fl.accumulate(1, vt_ref, rows(2 * i))
        fl.logits(1, 0, k_ref, rows(2 * i + 2))
        fl.accumulate(0, vt_ref, rows(2 * i + 1))
        return 0

    lax.fori_loop(0, n_past // 2, body, 0)

    @pl.when(n_past % 2 == 1)
    def _():
        fl.accumulate(1, vt_ref, rows(n_past - 1))

    o_ref[...] = fl.result().astype(o_ref.dtype)


def _fox_kernel(cend_ref, q_ref, k_ref, vt_ref, o_ref, knorm_ref, *flash_refs):
    t = q_ref.shape[0]
    hp = pl.program_id(0)
    qi = pl.program_id(1)
    fl = _Flash(*flash_refs)
    kt = fl.s.shape[1]
    n_tiles = k_ref.shape[0] // kt
    n_past = (qi * t) // kt

    @pl.when(qi == 0)
    def _():
        key_lanes = lax.broadcasted_iota(jnp.int32, (1, 2 * LANES), 1) % LANES < HEAD_DIM

        def step(b, best):
            k = k_ref[pl.ds(pl.multiple_of(b * kt, kt), kt), :].astype(F32)
            sq = jnp.where(key_lanes, k * k, 0.0)
            return tuple(jnp.maximum(best[a], jnp.sum(sq[:, a * LANES:(a + 1) * LANES], axis=1, keepdims=True))
                         for a in range(2))
        zero = jnp.zeros((kt, 1), F32)
        best = lax.fori_loop(0, n_tiles, step, (zero, zero))
        for a in range(2):
            knorm_ref[a:a + 1, :] = jnp.broadcast_to(jnp.max(best[a], axis=0, keepdims=True), (1, LANES))

    lane = lax.broadcasted_iota(jnp.int32, (t, LANES), 1)
    minus_one = jnp.where(jnp.logical_and(lane >= HEAD_DIM, lane < HEAD_DIM + 3), -1.0, 0.0).astype(q_ref.dtype)
    heads = _split_heads(q_ref)
    for a in range(2):
        fl.qs[a] = jnp.where(lane < HEAD_DIM, heads[a], minus_one)

    def rows(j):
        return pl.ds(pl.multiple_of(jnp.maximum(j, 0) * kt, kt), kt)

    fl.init()
    fl.logits(0, 0, k_ref, rows(n_past), _diagonal_mask(kt, t, qi * t - n_past * kt))
    fl.logits(1, 0, k_ref, rows(n_past - 1))
    fl.accumulate(0, vt_ref, rows(n_past))

    reach = []
    for a in range(2):
        q32 = heads[a].astype(F32).T
        qn2 = jnp.sum(q32 * q32, axis=0, keepdims=True)
        zmax = jnp.sqrt(qn2 * knorm_ref[a:a + 1, 0:1]) * 1.001
        reach.append(jnp.max(zmax - fl.m[a][0:1, :]))

    def alive(j):
        jj = jnp.maximum(j, 0)
        live = [reach[a] - cend_ref[(2 * hp + a) * n_tiles + jj] > (EXP_ZERO - BOUND_SLACK) * LOG2E
                for a in range(2)]
        return jnp.logical_and(j >= 0, jnp.logical_or(live[0], live[1]))

    def body(j):
        fl.logits(0, 0, k_ref, rows(j - 1))
        fl.accumulate(1, vt_ref, rows(j))
        fl.logits(1, 0, k_ref, rows(j - 2))
        fl.accumulate(0, vt_ref, rows(j - 1))
        return j - 2

    j = lax.while_loop(lambda j: alive(j - 1), body, n_past - 1)

    @pl.when(alive(j))
    def _():
        fl.accumulate(1, vt_ref, rows(j))

    o_ref[...] = fl.result().astype(o_ref.dtype)


def _sb_attention(qkv, q_block0, k_block0, v_block0, n_pairs):
    S = qkv.shape[0]
    t = min(ATT_TILE, S)
    assert S % t == 0
    return pl.pallas_call(
        _sb_kernel, grid=(n_pairs, S // t),
        in_specs=[pl.BlockSpec((t, LANES), lambda p, i: (i, q_block0 + p)),
                  pl.BlockSpec((S, LANES), lambda p, i: (0, k_block0 + p)),
                  pl.BlockSpec((S, LANES), lambda p, i: (0, v_block0 + p))],
        out_specs=pl.BlockSpec((t, LANES), lambda p, i: (i, p)),
        out_shape=jax.ShapeDtypeStruct((S, n_pairs * LANES), BF16),
        compiler_params=_params("arbitrary", "arbitrary"), name="sb",
    )(qkv, qkv, qkv)


def _flash_attention(kind, q_arr, q_block0, keys, values_t, n_pairs, c_end=None):
    S = q_arr.shape[0]
    t = min(FLASH_TILE[kind], S)
    kt = min(KEY_TILE[kind], S)
    assert S % kt == 0 and kt % t == 0 and t % MOBA_BLOCK == 0
    in_specs = [pl.BlockSpec((t, LANES), lambda p, i: (i, q_block0 + p)),
                pl.BlockSpec((S, 2 * LANES), lambda p, i: (0, p)),
                pl.BlockSpec((LANES, S), lambda p, i: (p, 0))]
    args = [q_arr, keys, values_t]
    if kind == "moba":
        body = _moba_kernel
        assert S // MOBA_BLOCK <= HEAD_DIM
        scratch = [pltpu.VMEM((LANES, 2 * LANES), F32)] + [pltpu.VMEM((LANES, 2 * LANES), BF16)] * 3
    else:
        body = _fox_kernel
        in_specs = [pl.BlockSpec(memory_space=pltpu.SMEM)] + in_specs
        args = [c_end] + args
        scratch = [pltpu.VMEM((8, LANES), F32)]
    return pl.pallas_call(
        body, grid=(n_pairs, S // t), in_specs=in_specs,
        out_specs=pl.BlockSpec((t, LANES), lambda p, i: (i, p)),
        out_shape=jax.ShapeDtypeStruct((S, n_pairs * LANES), BF16),
        scratch_shapes=scratch + _flash_scratch(t, kt),
        compiler_params=_params("arbitrary", "arbitrary"), name=kind,
    )(*args)


def _memkv_kernel(mem_ref, g_ref, w_ref, o_ref):
    o_ref[...] = _dot(_rms(mem_ref[...], g_ref[...]).astype(BF16), w_ref[...]).astype(o_ref.dtype)


def _memory_kv(mem, g, w_kv):
    M, D = mem.shape
    N = w_kv.shape[1]
    return pl.pallas_call(
        _memkv_kernel, out_shape=jax.ShapeDtypeStruct((M, N), BF16), name="memkv",
        compiler_params=pltpu.CompilerParams(vmem_limit_bytes=VMEM_LIMIT),
    )(mem, g.reshape(1, D), w_kv.astype(BF16))


def _post_kernel(*refs, n_att):
    att_refs = refs[:n_att]
    h_ref, wo_ref, g_ref, wq_ref, kv_ref, wxo_ref, o_ref = refs[n_att:]
    tm = h_ref.shape[0]
    xa_dim = wq_ref.shape[1]
    hd = xa_dim // XA_HEADS
    h1 = h_ref[...]
    col = 0
    for a_ref in att_refs:
        w = a_ref.shape[1]
        h1 = h1 + _dot(a_ref[...], wo_ref[col:col + w, :])
        col += w
    q = _dot(_rms(h1, g_ref[...]).astype(BF16), wq_ref[...]).astype(BF16)
    kv = kv_ref[...]
    k = kv[:, :xa_dim]
    v = kv[:, xa_dim:]
    lane = lax.broadcasted_iota(jnp.int32, (tm, xa_dim), 1)
    o = jnp.zeros((tm, xa_dim), F32)
    for hh in range(XA_HEADS):
        mine = (lane >= hh * hd) & (lane < (hh + 1) * hd)
        s = _dot_nt(jnp.where(mine, q, jnp.zeros_like(q)), k) * (hd ** -0.5)
        m = jnp.max(s, axis=1, keepdims=True)
        p = jnp.exp(s - m)
        l = jnp.sum(p, axis=1, keepdims=True)
        o = jnp.where(mine, _dot(p.astype(BF16), v) / l, o)
    o_ref[...] = h1 + _dot(o.astype(BF16), wxo_ref[...])


def _post(att_parts, h, w_out, g_xa, w_q, kv, w_xo):
    S, D = h.shape
    tm = min(ROW_TILE, S)
    row = lambda i: (i, 0)
    in_specs = [pl.BlockSpec((tm, a.shape[1]), row) for a in att_parts]
    in_specs += [pl.BlockSpec((tm, D), row), _resident(w_out.shape), _resident((1, D)),
                 _resident(w_q.shape), _resident(kv.shape), _resident(w_xo.shape)]
    return pl.pallas_call(
        functools.partial(_post_kernel, n_att=len(att_parts)),
        grid=(S // tm,), in_specs=in_specs, out_specs=pl.BlockSpec((tm, D), row),
        out_shape=jax.ShapeDtypeStruct((S, D), F32), compiler_params=_params("arbitrary"), name="post",
    )(*att_parts, h, w_out.astype(BF16), g_xa.reshape(1, D), w_q.astype(BF16), kv, w_xo.astype(BF16))


def _ffn_kernel(*refs, d_ff, final):
    if final:
        h_ref, g_ref, wup_ref, cw_ref, cb_ref, wdn_ref, gfin_ref, o_ref, prev_ref = refs
    else:
        h_ref, g_ref, wup_ref, cw_ref, cb_ref, wdn_ref, o_ref, prev_ref = refs
    tm = h_ref.shape[0]

    @pl.when(pl.program_id(0) == 0)
    def _():
        prev_ref[...] = jnp.zeros_like(prev_ref)

    h = h_ref[...]
    hn = _rms(h, g_ref[...]).astype(BF16)
    rowi = lax.broadcasted_iota(jnp.int32, (tm, FF_CHUNK), 0)

    def up(c):
        return [_dot(hn, wup_ref[:, col0:col0 + FF_CHUNK]) for col0 in (c * FF_CHUNK, d_ff + c * FF_CHUNK)]

    def conv(col0, u):
        cols = slice(col0, col0 + FF_CHUNK)
        prev = prev_ref[:, cols]
        p1, p2 = prev[7:8, :], prev[6:7, :]
        u1 = jnp.where(rowi == 0, p1, pltpu.roll(u, 1, 0))
        u2 = jnp.where(rowi == 0, p2, jnp.where(rowi == 1, p1, pltpu.roll(u, 2, 0)))
        prev_ref[:, cols] = u[tm - 8:tm, :]
        w = cw_ref[:, cols]
        return cb_ref[:, cols] + u2 * w[0:1, :] + u1 * w[1:2, :] + u * w[2:3, :]

    acc = h
    n_chunks = d_ff // FF_CHUNK
    u_next = up(0)
    for c in range(n_chunks):
        u_gate, u_val = u_next
        if c + 1 < n_chunks:
            u_next = up(c + 1)
        gate = conv(c * FF_CHUNK, u_gate)
        val = conv(d_ff + c * FF_CHUNK, u_val)
        act = gate * (1.0 / (1.0 + jnp.exp(-gate))) * val
        acc = acc + _dot(act.astype(BF16), wdn_ref[c * FF_CHUNK:(c + 1) * FF_CHUNK, :])
    if final:
        acc = _rms(acc, gfin_ref[...])
    o_ref[...] = acc


def _ffn(h, g, w_up, conv_w, conv_b, w_down, g_final=None):
    S, D = h.shape
    d_ff = w_down.shape[0]
    tm = min(ROW_TILE, S)
    assert d_ff % FF_CHUNK == 0 and conv_w.shape[0] == CONV_WIDTH
    row = lambda i: (i, 0)
    in_specs = [pl.BlockSpec((tm, D), row), _resident((1, D)), _resident(w_up.shape),
                _resident(conv_w.shape), _resident((1, 2 * d_ff)), _resident(w_down.shape)]
    args = [h, g.reshape(1, D), w_up.astype(BF16), conv_w, conv_b.reshape(1, 2 * d_ff), w_down.astype(BF16)]
    if g_final is not None:
        in_specs.append(_resident((1, D)))
        args.append(g_final.reshape(1, D))
    return pl.pallas_call(
        functools.partial(_ffn_kernel, d_ff=d_ff, final=g_final is not None),
        grid=(S // tm,), in_specs=in_specs, out_specs=pl.BlockSpec((tm, D), row),
        out_shape=jax.ShapeDtypeStruct((S, D), F32),
        scratch_shapes=[pltpu.VMEM((8, 2 * d_ff), F32)],
        compiler_params=_params("arbitrary"), name="ffn",
    )(*args)


def kernel(x, mem, positions, norm_mix_g, norm_xa_g, norm_mem_g, norm_ffn_g, ab_w_in, ab_w_out, fox_w_in, fox_b_f, fox_w_out, xa_w_q, xa_w_kv, xa_w_out, ffn_w_up, ffn_conv_w, ffn_conv_b, ffn_w_down, final_norm_g):
    B, S, D = x.shape
    depth = norm_mix_g.shape[0]
    n_pairs = D // LANES
    q_scale2 = HEAD_DIM ** -0.5 * LOG2E
    outs = []
    for b in range(B):
        h = x[b]
        for layer in range(depth):
            n = layer // 2
            if layer % 2 == 0:
                assert D == 2 * PROJ_CHUNK
                half = n_pairs // 2
                w_in = ab_w_in[n]
                qkv, k_moba, vt_moba = _project(
                    h, norm_mix_g[layer], w_in, w_in[:, 2 * D + PROJ_CHUNK:].T,
                    (1.0, q_scale2, 1.0, 1.0, 1.0, 1.0), range(6), (3,), rope=(positions[b], (1, 3)))
                att = [_sb_attention(qkv, 0, n_pairs, 2 * n_pairs, half),
                       _flash_attention("moba", qkv, half, k_moba, vt_moba, half)]
                w_out = ab_w_out[n]
            else:
                w_in = fox_w_in[n]
                q, keys, vt, c2 = _project(
                    h, norm_mix_g[layer], w_in[:, :2 * D], w_in[:, 2 * D:3 * D].T,
                    (q_scale2, q_scale2, 1.0, 1.0), (0, 1), (2, 3), gates=(w_in[:, 3 * D:], fox_b_f[n]))
                kt = min(KEY_TILE["fox"], S)
                c_end = c2[kt - 1::kt, :fox_b_f.shape[1]].T.reshape(-1)
                att = [_flash_attention("fox", q, 0, keys, vt, n_pairs, c_end)]
                w_out = fox_w_out[n]
            kv = _memory_kv(mem[b], norm_mem_g[layer], xa_w_kv[layer])
            h = _post(att, h, w_out, norm_xa_g[layer], xa_w_q[layer], kv, xa_w_out[layer])
            h = _ffn(h, norm_ffn_g[layer], ffn_w_up[layer], ffn_conv_w[layer], ffn_conv_b[layer],
                     ffn_w_down[layer], final_norm_g if layer == depth - 1 else None)
        outs.append(h)
    return jnp.stack(outs)
```

```python
import functools

import numpy as np
import jax
import jax.numpy as jnp
from jax import lax
from jax.experimental import pallas as pl
from jax.experimental.pallas import tpu as pltpu

F32 = jnp.float32
BF16 = jnp.bfloat16

HEAD_DIM = 64
LANES = 128
MOBA_BLOCK = 256
MOBA_TOPK = 3
ROPE_THETA = 10000.0
XA_HEADS = 4
CONV_WIDTH = 3
RMS_EPS = 1e-6
NEG_INF = -1e9
KNOCKOUT = -3.0e38
LOG2E = float(np.log2(np.e))
EXP_ZERO = -104.0
BOUND_SLACK = 1.0
ATT_TILE = 256
SB_GROUP = 2
FLASH_TILE = {"moba": 1024, "fox": 512}
KEY_TILE = {"moba": 1024, "fox": 512}
ROW_TILE = 512
PROJ_CHUNK = 512
FF_CHUNK = 256
VMEM_LIMIT = 56 * 1024 * 1024

_NT = (((1,), (1,)), ((), ()))


def _dot(a, b):
    return jnp.dot(a, b, preferred_element_type=F32)


def _dot_nt(a, b):
    return lax.dot_general(a, b, _NT, preferred_element_type=F32)


def _split2(x):
    hi = x.astype(BF16)
    lo = (x - hi.astype(F32)).astype(BF16)
    return hi, lo


def _split3(x):
    hi = x.astype(BF16)
    r = x - hi.astype(F32)
    mid = r.astype(BF16)
    lo = (r - mid.astype(F32)).astype(BF16)
    return hi, mid, lo


def _rms(x, g):
    return x * lax.rsqrt(jnp.mean(x * x, axis=-1, keepdims=True) + RMS_EPS) * g


def _log_sigmoid(z):
    return jnp.minimum(z, 0.0) - jnp.log(1.0 + jnp.exp(-jnp.abs(z)))


def _resident(shape):
    return pl.BlockSpec(shape, lambda *_: (0,) * len(shape), pipeline_mode=pl.Buffered(1))


def _params(*sem):
    return pltpu.CompilerParams(dimension_semantics=sem, vmem_limit_bytes=VMEM_LIMIT)


def _proj_kernel(*refs, chunk_scale, rope_chunks, out_chunks, key_chunks, gate_keys, n_vt):
    refs = list(refs)
    x_ref, g_ref, w_ref = refs[:3]
    del refs[:3]
    pos_ref = refs.pop(0) if rope_chunks else None
    wvt_ref = refs.pop(0)
    if gate_keys:
        wft_ref, bf_ref = refs[:2]
        del refs[:2]
    o_ref, kaug_ref, vt_ref = refs[:3]
    del refs[:3]
    if gate_keys:
        crow_ref, carry_ref = refs
    tm = x_ref.shape[0]
    hn32 = _rms(x_ref[...], g_ref[...])
    hn = hn32.astype(BF16)
    lane = lax.broadcasted_iota(jnp.int32, (tm, LANES), 1)
    low = lane < HEAD_DIM

    if rope_chunks:
        half = HEAD_DIM // 2
        first_half = (lane % HEAD_DIM) < half
        freq_id = (lax.broadcasted_iota(jnp.int32, (1, LANES), 1) % half).astype(F32)
        inv_freq = jnp.power(ROPE_THETA, -freq_id / half)
        ang = pos_ref[...].astype(F32) * inv_freq
        cos, sin = jnp.cos(ang), jnp.sin(ang)
        sin_signed = jnp.where(first_half, -sin, sin)

    if gate_keys:
        @pl.when(pl.program_id(0) == 0)
        def _():
            carry_ref[...] = jnp.zeros_like(carry_ref)

        h_hi, h_lo = _split2(hn32)
        w_hi, w_lo = _split2(wft_ref[...])
        f = _dot_nt(w_hi, h_hi) + _dot_nt(w_hi, h_lo) + _dot_nt(w_lo, h_hi) + bf_ref[...]
        r = lax.broadcasted_iota(jnp.int32, (tm, tm), 0)
        cidx = lax.broadcasted_iota(jnp.int32, (tm, tm), 1)
        incl = (r <= cidx).astype(BF16)
        c_row = carry_ref[...]
        for piece in _split3(_log_sigmoid(f)):
            c_row = c_row + _dot(piece, incl)
        carry_ref[...] = c_row[:, tm - 1:tm]
        c2 = c_row * LOG2E
        crow_ref[...] = c2
        n_heads = c2.shape[0]
        c2_col = jnp.concatenate([c2, jnp.zeros((LANES - n_heads, tm), F32)], axis=0).T
        extras = sum(pltpu.roll(piece.astype(F32), HEAD_DIM + 16 * i, 1) for i, piece in enumerate(_split3(c2_col)))
    elif key_chunks:
        row = pl.program_id(0) * tm + lax.broadcasted_iota(jnp.int32, (tm, LANES), 0)
        one_hot = (lane - HEAD_DIM == row // MOBA_BLOCK).astype(F32)

    for c in range(len(chunk_scale)):
        y = _dot(hn, w_ref[:, c * PROJ_CHUNK:(c + 1) * PROJ_CHUNK])
        parts = [y[:, s * LANES:(s + 1) * LANES] for s in range(PROJ_CHUNK // LANES)]
        if c in rope_chunks:
            parts = [ys * cos + sin_signed * jnp.where(first_half, pltpu.roll(ys, LANES - HEAD_DIM // 2, 1),
                                                       pltpu.roll(ys, HEAD_DIM // 2, 1)) for ys in parts]
        if c in key_chunks:
            base = key_chunks.index(c) * 2 * len(parts)
            for s, ys in enumerate(parts):
                for a, head in enumerate((ys, pltpu.roll(ys, HEAD_DIM, 1))):
                    col = (base + 2 * s + a) * LANES
                    upper = extras if gate_keys else one_hot
                    kaug_ref[:, col:col + LANES] = jnp.where(low, head, upper).astype(BF16)
        if c in out_chunks:
            y = jnp.concatenate(parts, axis=1)
            if chunk_scale[c] != 1.0:
                y = y * chunk_scale[c]
            col = out_chunks.index(c) * PROJ_CHUNK
            o_ref[:, col:col + PROJ_CHUNK] = y.astype(o_ref.dtype)

    for i in range(n_vt // PROJ_CHUNK):
        rows = slice(i * PROJ_CHUNK, (i + 1) * PROJ_CHUNK)
        vt_ref[rows, :] = _dot_nt(wvt_ref[rows, :], hn).astype(BF16)


def _project(x, g, w, w_vt, chunk_scale, out_chunks, key_chunks, *, rope=None, gates=None):
    S, D = x.shape
    tm = min(ROW_TILE, S)
    n_vt = w_vt.shape[0]
    n_key_lanes = len(key_chunks) * 2 * PROJ_CHUNK
    assert S % tm == 0 and w.shape[1] == PROJ_CHUNK * len(chunk_scale) and n_vt % PROJ_CHUNK == 0
    row = lambda i: (i, 0)
    in_specs = [pl.BlockSpec((tm, D), row), _resident((1, D)), _resident(w.shape)]
    args = [x, g.reshape(1, D), w.astype(BF16)]
    rope_chunks = ()
    if rope is not None:
        positions, rope_chunks = rope
        assert S // MOBA_BLOCK <= HEAD_DIM
        in_specs += [pl.BlockSpec((tm, 1), row)]
        args += [positions.reshape(S, 1)]
    in_specs += [_resident(w_vt.shape)]
    args += [w_vt.astype(BF16)]
    out_shape = [jax.ShapeDtypeStruct((S, len(out_chunks) * PROJ_CHUNK), BF16),
                 jax.ShapeDtypeStruct((S, n_key_lanes), BF16), jax.ShapeDtypeStruct((n_vt, S), BF16)]
    out_specs = [pl.BlockSpec((tm, len(out_chunks) * PROJ_CHUNK), row), pl.BlockSpec((tm, n_key_lanes), row),
                 pl.BlockSpec((n_vt, tm), lambda i: (0, i))]
    scratch = []
    if gates is not None:
        w_f, b_f = gates
        H = w_f.shape[1]
        assert H * LANES == n_key_lanes and H <= 16
        in_specs += [_resident((H, D)), _resident((H, 1))]
        args += [w_f.T, b_f.reshape(H, 1)]
        out_shape += [jax.ShapeDtypeStruct((H, S), F32)]
        out_specs += [pl.BlockSpec((H, tm), lambda i: (0, i))]
        scratch = [pltpu.VMEM((H, 1), F32)]
    return pl.pallas_call(
        functools.partial(_proj_kernel, chunk_scale=tuple(chunk_scale), rope_chunks=tuple(rope_chunks),
                          out_chunks=tuple(out_chunks), key_chunks=tuple(key_chunks),
                          gate_keys=gates is not None, n_vt=n_vt),
        grid=(S // tm,), in_specs=in_specs, out_specs=out_specs, out_shape=out_shape,
        scratch_shapes=scratch, compiler_params=_params("arbitrary"),
        name="proj_gates" if gates is not None else "proj_rope",
    )(*args)


def _head_lanes(shape, a):
    lane = lax.broadcasted_iota(jnp.int32, shape, len(shape) - 1)
    return (lane >= a * HEAD_DIM) & (lane < (a + 1) * HEAD_DIM)


def _kv_block(k_ref, v_ref, j, t):
    rows = pl.ds(pl.multiple_of(j * t, t), t)
    return k_ref[rows, :], v_ref[rows, :]


def _sb_kernel(q_ref, k_ref, v_ref, o_ref):
    t = ATT_TILE
    n_sub = q_ref.shape[0] // t
    step = pl.program_id(1)
    scale = HEAD_DIM ** -0.5
    r = lax.broadcasted_iota(jnp.int32, (t, t), 0)
    c = lax.broadcasted_iota(jnp.int32, (t, t), 1)
    strict = c < r
    later = (r > c).astype(BF16)
    q2 = q_ref[...] * jnp.asarray(scale, q_ref.dtype)
    lanes = [_head_lanes((t, LANES), a) for a in range(2)]
    chains = [(sub, a) for sub in range(n_sub) for a in range(2)]
    qs = [jnp.where(lanes[a], q2[sub * t:(sub + 1) * t, :], jnp.zeros((t, LANES), q2.dtype)) for sub, a in chains]

    def inner(qs, ks, diagonal):
        n = range(len(qs))
        z = [_dot_nt(qs[i], ks[i]) for i in n]
        lb = [_log_sigmoid(z[i]) for i in n]
        l1 = [lb[i] - z[i] for i in n]
        if diagonal:
            l1 = [jnp.where(strict, l1[i], 0.0) for i in n]
        parts = [_split2(l1[i]) for i in n]
        tail = [_dot(parts[i][0], later) + _dot(parts[i][1], later) for i in n]
        u = [lb[i] + tail[i] for i in n]
        if diagonal:
            u = [jnp.where(strict, u[i], NEG_INF) for i in n]
        return u, [tail[i][:, 0:1] + l1[i][:, 0:1] for i in n]

    def weigh(u, carry, vs):
        return [_dot(jnp.exp(u[i] + carry[i]).astype(BF16), vs[i]) for i in range(len(u))]

    tiles = [step * n_sub + sub for sub in range(n_sub)]
    own = [_kv_block(k_ref, v_ref, qi, t) for qi in tiles]
    prev = [_kv_block(k_ref, v_ref, jnp.maximum(qi - 1, 0), t) for qi in tiles]
    prev = [(k, v * jnp.where(qi > 0, 1.0, 0.0).astype(v.dtype)) for (k, v), qi in zip(prev, tiles)]
    u_own, sum_own = inner(qs, [own[sub][0] for sub, _ in chains], True)
    u_prev, sum_prev = inner(qs, [prev[sub][0] for sub, _ in chains], False)
    acc = weigh(u_own, [0.0] * len(chains), [own[sub][1] for sub, _ in chains])
    pv = weigh(u_prev, sum_own, [prev[sub][1] for sub, _ in chains])
    acc = [acc[i] + pv[i] for i in range(len(chains))]
    carry = [sum_own[i] + sum_prev[i] for i in range(len(chains))]

    for sub in range(n_sub):
        i0, i1 = 2 * sub, 2 * sub + 1

        def cond(st):
            kb, carry0, carry1, _, _ = st
            return jnp.logical_and(kb >= 0, jnp.max(jnp.maximum(carry0, carry1)) > EXP_ZERO)

        def body(st, i0=i0, i1=i1):
            kb, carry0, carry1, acc0, acc1 = st
            k, v = _kv_block(k_ref, v_ref, kb, t)
            u, sums = inner([qs[i0], qs[i1]], [k, k], False)
            pv = weigh(u, [carry0, carry1], [v, v])
            return kb - 1, carry0 + sums[0], carry1 + sums[1], acc0 + pv[0], acc1 + pv[1]

        _, _, _, acc0, acc1 = lax.while_loop(cond, body, (tiles[sub] - 2, carry[i0], carry[i1], acc[i0], acc[i1]))
        o_ref[sub * t:(sub + 1) * t, :] = jnp.where(lanes[0], acc0, acc1).astype(o_ref.dtype)


class _Flash:
    def __init__(self, qs, s, cm, m, l, acc):
        self.qs, self.s, self.cm, self.m, self.l, self.acc = qs, s, cm, m, l, acc

    def init(self):
        for a in range(2):
            self.m[a] = jnp.full(self.m.shape[1:], KNOCKOUT, F32)
            self.l[a] = jnp.zeros(self.l.shape[1:], F32)
            self.acc[a] = jnp.zeros(self.acc.shape[1:], F32)

    def logits(self, buf, q_slot0, k_ref, rows, mask=None):
        for a in range(2):
            s = _dot_nt(k_ref[rows, a * LANES:(a + 1) * LANES], self.qs[q_slot0 + a])
            if mask is not None:
                s = jnp.where(mask, s, NEG_INF)
            self.s[2 * buf + a] = s
            kt, t = s.shape
            top = jnp.max(jnp.max(s.reshape(kt // 8, 8, t), axis=0), axis=0, keepdims=True)
            self.cm[2 * buf + a] = jnp.broadcast_to(top, (8, t))

    def accumulate(self, buf, vt_ref, rows):
        for a in range(2):
            m_new = jnp.maximum(self.m[a], self.cm[2 * buf + a])
            alpha = jnp.exp2(self.m[a] - m_new)
            s = self.s[2 * buf + a]
            kt, t = s.shape
            p = jnp.exp2(s - m_new[0:1, :])
            self.l[a] = alpha * self.l[a] + jnp.sum(p.reshape(kt // 8, 8, t), axis=0)
            pv = _dot(vt_ref[a * HEAD_DIM:(a + 1) * HEAD_DIM, rows], p.astype(BF16))
            self.acc[a] = alpha[0:1, :] * self.acc[a] + pv
            self.m[a] = m_new

    def result(self):
        out = [self.acc[a] / jnp.sum(self.l[a], axis=0, keepdims=True) for a in range(2)]
        return jnp.concatenate(out, axis=0).T


def _flash_scratch(t, kt):
    return [pltpu.VMEM((4, t, LANES), BF16), pltpu.VMEM((4, kt, t), F32), pltpu.VMEM((4, 8, t), F32),
            pltpu.VMEM((2, 8, t), F32), pltpu.VMEM((2, 8, t), F32), pltpu.VMEM((2, HEAD_DIM, t), F32)]


def _split_heads(q_ref):
    q2 = q_ref[...]
    low = lax.broadcasted_iota(jnp.int32, q2.shape, 1) < HEAD_DIM
    moved = pltpu.roll(q2.astype(F32), HEAD_DIM, 1).astype(q2.dtype)
    return [jnp.where(low, q, jnp.zeros_like(q2)) for q in (q2, moved)]


def _diagonal_mask(kt, t, shift):
    key = lax.broadcasted_iota(jnp.int32, (kt, t), 0)
    query = lax.broadcasted_iota(jnp.int32, (kt, t), 1)
    return key <= query + shift


def _moba_kernel(q_ref, k_ref, vt_ref, o_ref, km32, km_hi, km_mid, km_lo, *flash_refs):
    t = q_ref.shape[0]
    nb = k_ref.shape[0] // MOBA_BLOCK
    qi = pl.program_id(1)
    fl = _Flash(*flash_refs)
    kt = fl.s.shape[1]
    n_tiles = k_ref.shape[0] // kt
    n_past = (qi * t) // kt

    @pl.when(qi == 0)
    def _():
        km32[...] = jnp.zeros_like(km32)

        def centroid(b, _):
            blk = k_ref[pl.ds(pl.multiple_of(b * MOBA_BLOCK, MOBA_BLOCK), MOBA_BLOCK), :].astype(F32)
            km32[pl.ds(HEAD_DIM + b, 1), :] = jnp.sum(blk, axis=0, keepdims=True) * (1.0 / MOBA_BLOCK)
            return 0
        lax.fori_loop(0, nb, centroid, 0)
        hi, mid, lo = _split3(km32[...])
        km_hi[...] = hi
        km_mid[...] = mid
        km_lo[...] = lo

    lane = lax.broadcasted_iota(jnp.int32, (t, LANES), 1)
    low = lane < HEAD_DIM
    blk = lane - HEAD_DIM
    own = (qi * t + lax.broadcasted_iota(jnp.int32, (t, LANES), 0)) // MOBA_BLOCK
    slot = lax.broadcasted_iota(jnp.int32, (LANES, t), 0)
    slot_f = slot.astype(F32)
    own_t = (qi * t + lax.broadcasted_iota(jnp.int32, (LANES, t), 1)) // MOBA_BLOCK
    candidate = jnp.logical_and(slot >= HEAD_DIM, slot - HEAD_DIM < own_t)
    for a, q in enumerate(_split_heads(q_ref)):
        cols = slice(a * LANES, (a + 1) * LANES)
        gate = _dot_nt(km_hi[:, cols], q) + _dot_nt(km_mid[:, cols], q) + _dot_nt(km_lo[:, cols], q)
        g = jnp.where(candidate, gate, NEG_INF)
        sel_t = jnp.zeros((LANES, t), F32)
        for rank in range(MOBA_TOPK):
            best = jnp.max(g, axis=0, keepdims=True)
            idx = jnp.min(jnp.where(g == best, slot_f, float(LANES)), axis=0, keepdims=True)
            pick = slot_f == idx
            sel_t = jnp.where(pick, jnp.where(rank < own_t, 1.0, 0.0), sel_t)
            g = jnp.where(pick, KNOCKOUT, g)
        sel = sel_t.T
        past_bias = jnp.where(sel > 0.0, 0.0, NEG_INF).astype(q.dtype)
        own_bias = jnp.where(jnp.logical_or(sel > 0.0, blk == own), 0.0, NEG_INF).astype(q.dtype)
        fl.qs[a] = jnp.where(low, q, past_bias)
        fl.qs[2 + a] = jnp.where(low, q, own_bias)

    def rows(j):
        return pl.ds(pl.multiple_of(jnp.minimum(j, n_tiles - 1) * kt, kt), kt)

    fl.init()
    fl.logits(0, 2, k_ref, rows(n_past), _diagonal_mask(kt, t, qi * t - n_past * kt))
    fl.logits(1, 0, k_ref, rows(0))
    fl.accumulate(0, vt_ref, rows(n_past))

    def body(i, _):
        fl.logits(0, 0, k_ref, rows(2 * i + 1))
        fl.accumulate(1, vt_ref, rows(2 * i))
        fl.logits(1, 0, k_ref, rows(2 * i + 2))
        fl.accumulate(0, vt_ref, rows(2 * i + 1))
        return 0

    lax.fori_loop(0, n_past // 2, body, 0)

    @pl.when(n_past % 2 == 1)
    def _():
        fl.accumulate(1, vt_ref, rows(n_past - 1))

    o_ref[...] = fl.result().astype(o_ref.dtype)


def _fox_kernel(cend_ref, q_ref, k_ref, vt_ref, o_ref, knorm_ref, *flash_refs):
    t = q_ref.shape[0]
    hp = pl.program_id(0)
    qi = pl.program_id(1)
    fl = _Flash(*flash_refs)
    kt = fl.s.shape[1]
    n_tiles = k_ref.shape[0] // kt
    n_past = (qi * t) // kt

    @pl.when(qi == 0)
    def _():
        key_lanes = lax.broadcasted_iota(jnp.int32, (1, 2 * LANES), 1) % LANES < HEAD_DIM

        def step(b, best):
            k = k_ref[pl.ds(pl.multiple_of(b * kt, kt), kt), :].astype(F32)
            sq = jnp.where(key_lanes, k * k, 0.0)
            return tuple(jnp.maximum(best[a], jnp.sum(sq[:, a * LANES:(a + 1) * LANES], axis=1, keepdims=True))
                         for a in range(2))
        zero = jnp.zeros((kt, 1), F32)
        best = lax.fori_loop(0, n_tiles, step, (zero, zero))
        for a in range(2):
            knorm_ref[a:a + 1, :] = jnp.broadcast_to(jnp.max(best[a], axis=0, keepdims=True), (1, LANES))

    lane = lax.broadcasted_iota(jnp.int32, (t, LANES), 1)
    heads = _split_heads(q_ref)
    for a in range(2):
        piece_lane = lane - (HEAD_DIM + 2 * hp + a)
        on_piece = jnp.logical_or(piece_lane == 0, jnp.logical_or(piece_lane == 16, piece_lane == 32))
        minus_one = jnp.where(on_piece, -1.0, 0.0).astype(q_ref.dtype)
        fl.qs[a] = jnp.where(lane < HEAD_DIM, heads[a], minus_one)

    def rows(j):
        return pl.ds(pl.multiple_of(jnp.maximum(j, 0) * kt, kt), kt)

    fl.init()
    fl.logits(0, 0, k_ref, rows(n_past), _diagonal_mask(kt, t, qi * t - n_past * kt))
    fl.logits(1, 0, k_ref, rows(n_past - 1))
    fl.accumulate(0, vt_ref, rows(n_past))

    reach = []
    for a in range(2):
        q32 = heads[a].astype(F32).T
        qn2 = jnp.sum(q32 * q32, axis=0, keepdims=True)
        zmax = jnp.sqrt(qn2 * knorm_ref[a:a + 1, 0:1]) * 1.001
        reach.append(jnp.max(zmax - fl.m[a][0:1, :]))

    def alive(j):
        jj = jnp.maximum(j, 0)
        live = [reach[a] - cend_ref[(2 * hp + a) * n_tiles + jj] > (EXP_ZERO - BOUND_SLACK) * LOG2E
                for a in range(2)]
        return jnp.logical_and(j >= 0, jnp.logical_or(live[0], live[1]))

    def body(j):
        fl.logits(0, 0, k_ref, rows(j - 1))
        fl.accumulate(1, vt_ref, rows(j))
        fl.logits(1, 0, k_ref, rows(j - 2))
        fl.accumulate(0, vt_ref, rows(j - 1))
        return j - 2

    j = lax.while_loop(lambda j: alive(j - 1), body, n_past - 1)

    @pl.when(alive(j))
    def _():
        fl.accumulate(1, vt_ref, rows(j))

    o_ref[...] = fl.result().astype(o_ref.dtype)


def _sb_attention(qkv, q_block0, k_block0, v_block0, n_pairs):
    S = qkv.shape[0]
    t = SB_GROUP * ATT_TILE
    assert S % t == 0
    return pl.pallas_call(
        _sb_kernel, grid=(n_pairs, S // t),
        in_specs=[pl.BlockSpec((t, LANES), lambda p, i: (i, q_block0 + p)),
                  pl.BlockSpec((S, LANES), lambda p, i: (0, k_block0 + p)),
                  pl.BlockSpec((S, LANES), lambda p, i: (0, v_block0 + p))],
        out_specs=pl.BlockSpec((t, LANES), lambda p, i: (i, p)),
        out_shape=jax.ShapeDtypeStruct((S, n_pairs * LANES), BF16),
        compiler_params=_params("arbitrary", "arbitrary"), name="sb",
    )(qkv, qkv, qkv)


def _flash_attention(kind, q_arr, q_block0, keys, values_t, n_pairs, c_end=None):
    S = q_arr.shape[0]
    t = min(FLASH_TILE[kind], S)
    kt = min(KEY_TILE[kind], S)
    assert S % kt == 0 and kt % t == 0 and t % MOBA_BLOCK == 0
    in_specs = [pl.BlockSpec((t, LANES), lambda p, i: (i, q_block0 + p)),
                pl.BlockSpec((S, 2 * LANES), lambda p, i: (0, p)),
                pl.BlockSpec((LANES, S), lambda p, i: (p, 0))]
    args = [q_arr, keys, values_t]
    if kind == "moba":
        body = _moba_kernel
        assert S // MOBA_BLOCK <= HEAD_DIM
        scratch = [pltpu.VMEM((LANES, 2 * LANES), F32)] + [pltpu.VMEM((LANES, 2 * LANES), BF16)] * 3
    else:
        body = _fox_kernel
        in_specs = [pl.BlockSpec(memory_space=pltpu.SMEM)] + in_specs
        args = [c_end] + args
        scratch = [pltpu.VMEM((8, LANES), F32)]
    return pl.pallas_call(
        body, grid=(n_pairs, S // t), in_specs=in_specs,
        out_specs=pl.BlockSpec((t, LANES), lambda p, i: (i, p)),
        out_shape=jax.ShapeDtypeStruct((S, n_pairs * LANES), BF16),
        scratch_shapes=scratch + _flash_scratch(t, kt),
        compiler_params=_params("arbitrary", "arbitrary"), name=kind,
    )(*args)


def _memkv_kernel(mem_ref, g_ref, w_ref, o_ref):
    o_ref[...] = _dot(_rms(mem_ref[...], g_ref[...]).astype(BF16), w_ref[...]).astype(o_ref.dtype)


def _memory_kv(mem, g, w_kv):
    M, D = mem.shape
    N = w_kv.shape[1]
    return pl.pallas_call(
        _memkv_kernel, out_shape=jax.ShapeDtypeStruct((M, N), BF16), name="memkv",
        compiler_params=pltpu.CompilerParams(vmem_limit_bytes=VMEM_LIMIT),
    )(mem, g.reshape(1, D), w_kv.astype(BF16))


def _post_kernel(*refs, n_att):
    att_refs = refs[:n_att]
    h_ref, wo_ref, g_ref, wq_ref, kv_ref, wxo_ref, o_ref = refs[n_att:]
    tm = h_ref.shape[0]
    xa_dim = wq_ref.shape[1]
    hd = xa_dim // XA_HEADS
    h1 = h_ref[...]
    col = 0
    for a_ref in att_refs:
        w = a_ref.shape[1]
        h1 = h1 + _dot(a_ref[...], wo_ref[col:col + w, :])
        col += w
    q = _dot(_rms(h1, g_ref[...]).astype(BF16), wq_ref[...]).astype(BF16)
    kv = kv_ref[...]
    k = kv[:, :xa_dim]
    v = kv[:, xa_dim:]
    lane = lax.broadcasted_iota(jnp.int32, (tm, xa_dim), 1)
    o = jnp.zeros((tm, xa_dim), F32)
    for hh in range(XA_HEADS):
        mine = (lane >= hh * hd) & (lane < (hh + 1) * hd)
        s = _dot_nt(jnp.where(mine, q, jnp.zeros_like(q)), k) * (hd ** -0.5)
        m = jnp.max(s, axis=1, keepdims=True)
        p = jnp.exp(s - m)
        l = jnp.sum(p, axis=1, keepdims=True)
        o = jnp.where(mine, _dot(p.astype(BF16), v) / l, o)
    o_ref[...] = h1 + _dot(o.astype(BF16), wxo_ref[...])


def _post(att_parts, h, w_out, g_xa, w_q, kv, w_xo):
    S, D = h.shape
    tm = min(ROW_TILE, S)
    row = lambda i: (i, 0)
    in_specs = [pl.BlockSpec((tm, a.shape[1]), row) for a in att_parts]
    in_specs += [pl.BlockSpec((tm, D), row), _resident(w_out.shape), _resident((1, D)),
                 _resident(w_q.shape), _resident(kv.shape), _resident(w_xo.shape)]
    return pl.pallas_call(
        functools.partial(_post_kernel, n_att=len(att_parts)),
        grid=(S // tm,), in_specs=in_specs, out_specs=pl.BlockSpec((tm, D), row),
        out_shape=jax.ShapeDtypeStruct((S, D), F32), compiler_params=_params("arbitrary"), name="post",
    )(*att_parts, h, w_out.astype(BF16), g_xa.reshape(1, D), w_q.astype(BF16), kv, w_xo.astype(BF16))


def _ffn_kernel(*refs, d_ff, final):
    if final:
        h_ref, g_ref, wup_ref, cw_ref, cb_ref, wdn_ref, gfin_ref, o_ref, prev_ref = refs
    else:
        h_ref, g_ref, wup_ref, cw_ref, cb_ref, wdn_ref, o_ref, prev_ref = refs
    tm = h_ref.shape[0]

    @pl.when(pl.program_id(0) == 0)
    def _():
        prev_ref[...] = jnp.zeros_like(prev_ref)

    h = h_ref[...]
    hn = _rms(h, g_ref[...]).astype(BF16)
    rowi = lax.broadcasted_iota(jnp.int32, (tm, FF_CHUNK), 0)

    def up(c):
        return [_dot(hn, wup_ref[:, col0:col0 + FF_CHUNK]) for col0 in (c * FF_CHUNK, d_ff + c * FF_CHUNK)]

    def conv(col0, u):
        cols = slice(col0, col0 + FF_CHUNK)
        prev = prev_ref[:, cols]
        p1, p2 = prev[7:8, :], prev[6:7, :]
        u1 = jnp.where(rowi == 0, p1, pltpu.roll(u, 1, 0))
        u2 = jnp.where(rowi == 0, p2, jnp.where(rowi == 1, p1, pltpu.roll(u, 2, 0)))
        prev_ref[:, cols] = u[tm - 8:tm, :]
        w = cw_ref[:, cols]
        return cb_ref[:, cols] + u2 * w[0:1, :] + u1 * w[1:2, :] + u * w[2:3, :]

    n_chunks = d_ff // FF_CHUNK
    u_next = up(0)
    down = None
    for c in range(n_chunks):
        u_gate, u_val = u_next
        if c + 1 < n_chunks:
            u_next = up(c + 1)
        gate = conv(c * FF_CHUNK, u_gate)
        val = conv(d_ff + c * FF_CHUNK, u_val)
        act = gate * (1.0 / (1.0 + jnp.exp(-gate))) * val
        part = _dot(act.astype(BF16), wdn_ref[c * FF_CHUNK:(c + 1) * FF_CHUNK, :])
        down = part if down is None else down + part
    acc = h + down
    if final:
        acc = _rms(acc, gfin_ref[...])
    o_ref[...] = acc


def _ffn(h, g, w_up, conv_w, conv_b, w_down, g_final=None):
    S, D = h.shape
    d_ff = w_down.shape[0]
    tm = min(ROW_TILE, S)
    assert d_ff % FF_CHUNK == 0 and conv_w.shape[0] == CONV_WIDTH
    row = lambda i: (i, 0)
    in_specs = [pl.BlockSpec((tm, D), row), _resident((1, D)), _resident(w_up.shape),
                _resident(conv_w.shape), _resident((1, 2 * d_ff)), _resident(w_down.shape)]
    args = [h, g.reshape(1, D), w_up.astype(BF16), conv_w, conv_b.reshape(1, 2 * d_ff), w_down.astype(BF16)]
    if g_final is not None:
        in_specs.append(_resident((1, D)))
        args.append(g_final.reshape(1, D))
    return pl.pallas_call(
        functools.partial(_ffn_kernel, d_ff=d_ff, final=g_final is not None),
        grid=(S // tm,), in_specs=in_specs, out_specs=pl.BlockSpec((tm, D), row),
        out_shape=jax.ShapeDtypeStruct((S, D), F32),
        scratch_shapes=[pltpu.VMEM((8, 2 * d_ff), F32)],
        compiler_params=_params("arbitrary"), name="ffn",
    )(*args)


def kernel(x, mem, positions, norm_mix_g, norm_xa_g, norm_mem_g, norm_ffn_g, ab_w_in, ab_w_out, fox_w_in, fox_b_f, fox_w_out, xa_w_q, xa_w_kv, xa_w_out, ffn_w_up, ffn_conv_w, ffn_conv_b, ffn_w_down, final_norm_g):
    B, S, D = x.shape
    depth = norm_mix_g.shape[0]
    n_pairs = D // LANES
    q_scale2 = HEAD_DIM ** -0.5 * LOG2E
    outs = []
    for b in range(B):
        h = x[b]
        for layer in range(depth):
            n = layer // 2
            if layer % 2 == 0:
                assert D == 2 * PROJ_CHUNK
                half = n_pairs // 2
                w_in = ab_w_in[n]
                qkv, k_moba, vt_moba = _project(
                    h, norm_mix_g[layer], w_in, w_in[:, 2 * D + PROJ_CHUNK:].T,
                    (1.0, q_scale2, 1.0, 1.0, 1.0, 1.0), range(6), (3,), rope=(positions[b], (1, 3)))
                att = [_sb_attention(qkv, 0, n_pairs, 2 * n_pairs, half),
                       _flash_attention("moba", qkv, half, k_moba, vt_moba, half)]
                w_out = ab_w_out[n]
            else:
                w_in = fox_w_in[n]
                q, keys, vt, c2 = _project(
                    h, norm_mix_g[layer], w_in[:, :2 * D], w_in[:, 2 * D:3 * D].T,
                    (q_scale2, q_scale2, 1.0, 1.0), (0, 1), (2, 3), gates=(w_in[:, 3 * D:], fox_b_f[n]))
                kt = min(KEY_TILE["fox"], S)
                c_end = c2[:, kt - 1::kt].reshape(-1)
                att = [_flash_attention("fox", q, 0, keys, vt, n_pairs, c_end)]
                w_out = fox_w_out[n]
            kv = _memory_kv(mem[b], norm_mem_g[layer], xa_w_kv[layer])
            h = _post(att, h, w_out, norm_xa_g[layer], xa_w_q[layer], kv, xa_w_out[layer])
            h = _ffn(h, norm_ffn_g[layer], ffn_w_up[layer], ffn_conv_w[layer], ffn_conv_b[layer],
                     ffn_w_down[layer], final_norm_g if layer == depth - 1 else None)
        outs.append(h)
    return jnp.stack(outs)
```

```python
import functools

import numpy as np
import jax
import jax.numpy as jnp
from jax import lax
from jax.experimental import pallas as pl
from jax.experimental.pallas import tpu as pltpu

F32 = jnp.float32
BF16 = jnp.bfloat16

HEAD_DIM = 64
LANES = 128
MOBA_BLOCK = 256
MOBA_TOPK = 3
ROPE_THETA = 10000.0
XA_HEADS = 4
CONV_WIDTH = 3
RMS_EPS = 1e-6
NEG_INF = -1e9
KNOCKOUT = -3.0e38
LOG2E = float(np.log2(np.e))
EXP_ZERO = -104.0
BOUND_SLACK = 1.0
ATT_TILE = 256
SB_GROUP = 4
ONES_ROWS = 16
GATE_LANES = 16
FLASH_TILE = {"moba": 1024, "fox": 512}
KEY_TILE = {"moba": 1024, "fox": 512}
ROW_TILE = 512
PROJ_CHUNK = 512
FF_CHUNK = 256
VMEM_LIMIT = 56 * 1024 * 1024

_NT = (((1,), (1,)), ((), ()))


def _dot(a, b):
    return jnp.dot(a, b, preferred_element_type=F32)


def _dot_nt(a, b):
    return lax.dot_general(a, b, _NT, preferred_element_type=F32)


def _split2(x):
    hi = x.astype(BF16)
    lo = (x - hi.astype(F32)).astype(BF16)
    return hi, lo


def _split3(x):
    hi = x.astype(BF16)
    r = x - hi.astype(F32)
    mid = r.astype(BF16)
    lo = (r - mid.astype(F32)).astype(BF16)
    return hi, mid, lo


def _rms(x, g):
    return x * lax.rsqrt(jnp.mean(x * x, axis=-1, keepdims=True) + RMS_EPS) * g


def _log_sigmoid(z):
    return jnp.minimum(z, 0.0) - jnp.log(1.0 + jnp.exp(-jnp.abs(z)))


def _resident(shape):
    return pl.BlockSpec(shape, lambda *_: (0,) * len(shape), pipeline_mode=pl.Buffered(1))


def _params(*sem):
    return pltpu.CompilerParams(dimension_semantics=sem, vmem_limit_bytes=VMEM_LIMIT)


def _proj_kernel(*refs, chunk_scale, rope_chunks, out_chunks, key_chunks, gate_keys, n_vt):
    refs = list(refs)
    x_ref, g_ref, w_ref = refs[:3]
    del refs[:3]
    pos_ref = refs.pop(0) if rope_chunks else None
    wvt_ref = refs.pop(0)
    if gate_keys:
        wft_ref, bf_ref = refs[:2]
        del refs[:2]
    o_ref, kaug_ref, vt_ref = refs[:3]
    del refs[:3]
    if gate_keys:
        crow_ref, carry_ref = refs
    tm = x_ref.shape[0]
    hn32 = _rms(x_ref[...], g_ref[...])
    hn = hn32.astype(BF16)
    lane = lax.broadcasted_iota(jnp.int32, (tm, LANES), 1)
    low = lane < HEAD_DIM

    if rope_chunks:
        half = HEAD_DIM // 2
        first_half = (lane % HEAD_DIM) < half
        freq_id = (lax.broadcasted_iota(jnp.int32, (1, LANES), 1) % half).astype(F32)
        inv_freq = jnp.power(ROPE_THETA, -freq_id / half)
        ang = pos_ref[...].astype(F32) * inv_freq
        cos, sin = jnp.cos(ang), jnp.sin(ang)
        sin_signed = jnp.where(first_half, -sin, sin)

    if gate_keys:
        @pl.when(pl.program_id(0) == 0)
        def _():
            carry_ref[...] = jnp.zeros_like(carry_ref)

        h_hi, h_lo = _split2(hn32)
        w_hi, w_lo = _split2(wft_ref[...])
        f = _dot_nt(w_hi, h_hi) + _dot_nt(w_hi, h_lo) + _dot_nt(w_lo, h_hi) + bf_ref[...]
        r = lax.broadcasted_iota(jnp.int32, (tm, tm), 0)
        cidx = lax.broadcasted_iota(jnp.int32, (tm, tm), 1)
        incl = (r <= cidx).astype(BF16)
        c_row = carry_ref[...]
        for piece in _split3(_log_sigmoid(f)):
            c_row = c_row + _dot(piece, incl)
        carry_ref[...] = c_row[:, tm - 1:tm]
        c2 = c_row * LOG2E
        crow_ref[...] = c2
        n_heads = c2.shape[0]
        c2_col = jnp.concatenate([c2, jnp.zeros((LANES - n_heads, tm), F32)], axis=0).T
        extras = sum(pltpu.roll(piece.astype(F32), HEAD_DIM + GATE_LANES * i, 1)
                     for i, piece in enumerate(_split3(c2_col)))
    elif key_chunks:
        row = pl.program_id(0) * tm + lax.broadcasted_iota(jnp.int32, (tm, LANES), 0)
        one_hot = (lane - HEAD_DIM == row // MOBA_BLOCK).astype(F32)

    for c in range(len(chunk_scale)):
        y = _dot(hn, w_ref[:, c * PROJ_CHUNK:(c + 1) * PROJ_CHUNK])
        parts = [y[:, s * LANES:(s + 1) * LANES] for s in range(PROJ_CHUNK // LANES)]
        if c in rope_chunks:
            parts = [ys * cos + sin_signed * jnp.where(first_half, pltpu.roll(ys, LANES - HEAD_DIM // 2, 1),
                                                       pltpu.roll(ys, HEAD_DIM // 2, 1)) for ys in parts]
        if c in key_chunks:
            base = key_chunks.index(c) * 2 * len(parts)
            for s, ys in enumerate(parts):
                for a, head in enumerate((ys, pltpu.roll(ys, HEAD_DIM, 1))):
                    col = (base + 2 * s + a) * LANES
                    upper = extras if gate_keys else one_hot
                    kaug_ref[:, col:col + LANES] = jnp.where(low, head, upper).astype(BF16)
        if c in out_chunks:
            y = jnp.concatenate(parts, axis=1)
            if chunk_scale[c] != 1.0:
                y = y * chunk_scale[c]
            col = out_chunks.index(c) * PROJ_CHUNK
            o_ref[:, col:col + PROJ_CHUNK] = y.astype(o_ref.dtype)

    for i in range(n_vt // PROJ_CHUNK):
        rows = slice(i * PROJ_CHUNK, (i + 1) * PROJ_CHUNK)
        vt_ref[rows, :] = _dot_nt(wvt_ref[rows, :], hn).astype(BF16)


def _project(x, g, w, w_vt, chunk_scale, out_chunks, key_chunks, *, rope=None, gates=None):
    S, D = x.shape
    tm = min(ROW_TILE, S)
    n_vt = w_vt.shape[0]
    n_key_lanes = len(key_chunks) * 2 * PROJ_CHUNK
    assert S % tm == 0 and w.shape[1] == PROJ_CHUNK * len(chunk_scale) and n_vt % PROJ_CHUNK == 0
    row = lambda i: (i, 0)
    in_specs = [pl.BlockSpec((tm, D), row), _resident((1, D)), _resident(w.shape)]
    args = [x, g.reshape(1, D), w.astype(BF16)]
    rope_chunks = ()
    if rope is not None:
        positions, rope_chunks = rope
        assert S // MOBA_BLOCK <= HEAD_DIM
        in_specs += [pl.BlockSpec((tm, 1), row)]
        args += [positions.reshape(S, 1)]
    in_specs += [_resident(w_vt.shape)]
    args += [w_vt.astype(BF16)]
    out_shape = [jax.ShapeDtypeStruct((S, len(out_chunks) * PROJ_CHUNK), BF16),
                 jax.ShapeDtypeStruct((S, n_key_lanes), BF16), jax.ShapeDtypeStruct((n_vt, S), BF16)]
    out_specs = [pl.BlockSpec((tm, len(out_chunks) * PROJ_CHUNK), row), pl.BlockSpec((tm, n_key_lanes), row),
                 pl.BlockSpec((n_vt, tm), lambda i: (0, i))]
    scratch = []
    if gates is not None:
        w_f, b_f = gates
        H = w_f.shape[1]
        assert H * LANES == n_key_lanes and H <= GATE_LANES
        in_specs += [_resident((H, D)), _resident((H, 1))]
        args += [w_f.T, b_f.reshape(H, 1)]
        out_shape += [jax.ShapeDtypeStruct((H, S), F32)]
        out_specs += [pl.BlockSpec((H, tm), lambda i: (0, i))]
        scratch = [pltpu.VMEM((H, 1), F32)]
    return pl.pallas_call(
        functools.partial(_proj_kernel, chunk_scale=tuple(chunk_scale), rope_chunks=tuple(rope_chunks),
                          out_chunks=tuple(out_chunks), key_chunks=tuple(key_chunks),
                          gate_keys=gates is not None, n_vt=n_vt),
        grid=(S // tm,), in_specs=in_specs, out_specs=out_specs, out_shape=out_shape,
        scratch_shapes=scratch, compiler_params=_params("arbitrary"),
        name="proj_gates" if gates is not None else "proj_rope",
    )(*args)


def _head_lanes(shape, a):
    lane = lax.broadcasted_iota(jnp.int32, shape, len(shape) - 1)
    return (lane >= a * HEAD_DIM) & (lane < (a + 1) * HEAD_DIM)


def _kv_block(k_ref, v_ref, j, t):
    rows = pl.ds(pl.multiple_of(j * t, t), t)
    return k_ref[rows, :], v_ref[rows, :]


def _sb_kernel(q_ref, k_ref, v_ref, o_ref):
    t = ATT_TILE
    n_sub = q_ref.shape[0] // t
    step = pl.program_id(1)
    scale = HEAD_DIM ** -0.5
    r = lax.broadcasted_iota(jnp.int32, (t, t), 0)
    c = lax.broadcasted_iota(jnp.int32, (t, t), 1)
    strict = c < r
    later = (r > c).astype(BF16)
    q2 = q_ref[...] * jnp.asarray(scale, q_ref.dtype)
    lanes = [_head_lanes((t, LANES), a) for a in range(2)]
    chains = [(sub, a) for sub in range(n_sub) for a in range(2)]
    qs = [jnp.where(lanes[a], q2[sub * t:(sub + 1) * t, :], jnp.zeros((t, LANES), q2.dtype)) for sub, a in chains]

    def inner(qs, ks, diagonal):
        n = range(len(qs))
        z = [_dot_nt(qs[i], ks[i]) for i in n]
        lb = [_log_sigmoid(z[i]) for i in n]
        l1 = [lb[i] - z[i] for i in n]
        if diagonal:
            l1 = [jnp.where(strict, l1[i], 0.0) for i in n]
        parts = [_split2(l1[i]) for i in n]
        tail = [_dot(parts[i][0], later) + _dot(parts[i][1], later) for i in n]
        u = [lb[i] + tail[i] for i in n]
        if diagonal:
            u = [jnp.where(strict, u[i], NEG_INF) for i in n]
        return u, [tail[i][:, 0:1] + l1[i][:, 0:1] for i in n]

    def weigh(u, carry, vs):
        return [_dot(jnp.exp(u[i] + carry[i]).astype(BF16), vs[i]) for i in range(len(u))]

    tiles = [step * n_sub + sub for sub in range(n_sub)]
    own = [_kv_block(k_ref, v_ref, qi, t) for qi in tiles]
    prev = [_kv_block(k_ref, v_ref, jnp.maximum(qi - 1, 0), t) for qi in tiles]
    prev = [(k, v * jnp.where(qi > 0, 1.0, 0.0).astype(v.dtype)) for (k, v), qi in zip(prev, tiles)]
    u_own, sum_own = inner(qs, [own[sub][0] for sub, _ in chains], True)
    u_prev, sum_prev = inner(qs, [prev[sub][0] for sub, _ in chains], False)
    acc = weigh(u_own, [0.0] * len(chains), [own[sub][1] for sub, _ in chains])
    pv = weigh(u_prev, sum_own, [prev[sub][1] for sub, _ in chains])
    acc = [acc[i] + pv[i] for i in range(len(chains))]
    carry = [sum_own[i] + sum_prev[i] for i in range(len(chains))]

    for sub in range(n_sub):
        i0, i1 = 2 * sub, 2 * sub + 1

        def cond(st):
            kb, carry0, carry1, _, _ = st
            return jnp.logical_and(kb >= 0, jnp.max(jnp.maximum(carry0, carry1)) > EXP_ZERO)

        def body(st, i0=i0, i1=i1):
            kb, carry0, carry1, acc0, acc1 = st
            k, v = _kv_block(k_ref, v_ref, kb, t)
            u, sums = inner([qs[i0], qs[i1]], [k, k], False)
            pv = weigh(u, [carry0, carry1], [v, v])
            return kb - 1, carry0 + sums[0], carry1 + sums[1], acc0 + pv[0], acc1 + pv[1]

        _, _, _, acc0, acc1 = lax.while_loop(cond, body, (tiles[sub] - 2, carry[i0], carry[i1], acc[i0], acc[i1]))
        o_ref[sub * t:(sub + 1) * t, :] = jnp.where(lanes[0], acc0, acc1).astype(o_ref.dtype)


class _Flash:
    def __init__(self, refs, mxu_sums):
        self.qs, self.s, self.cm, self.m, self.l, self.acc = refs
        self.mxu_sums = mxu_sums

    def init(self):
        for a in range(2):
            self.m[a] = jnp.full(self.m.shape[1:], KNOCKOUT, F32)
            self.l[a] = jnp.zeros(self.l.shape[1:], F32)
            self.acc[a] = jnp.zeros(self.acc.shape[1:], F32)

    def logits(self, buf, q_slot0, k_ref, rows, mask=None):
        for a in range(2):
            s = _dot_nt(k_ref[rows, a * LANES:(a + 1) * LANES], self.qs[q_slot0 + a])
            if mask is not None:
                s = jnp.where(mask, s, NEG_INF)
            self.s[2 * buf + a] = s
            kt, t = s.shape
            top = jnp.max(jnp.max(s.reshape(kt // 8, 8, t), axis=0), axis=0, keepdims=True)
            self.cm[2 * buf + a] = jnp.broadcast_to(top, (8, t))

    def accumulate(self, buf, vt_ref, rows):
        for a in range(2):
            m_new = jnp.maximum(self.m[a], self.cm[2 * buf + a])
            alpha = jnp.exp2(self.m[a] - m_new)
            s = self.s[2 * buf + a]
            kt, t = s.shape
            p = jnp.exp2(s - m_new[0:1, :])
            vt = vt_ref[a * HEAD_DIM:(a + 1) * HEAD_DIM, rows]
            if self.mxu_sums:
                pv = _dot(jnp.concatenate([vt, jnp.ones((ONES_ROWS, kt), BF16)], axis=0), p.astype(BF16))
                sums = pv[HEAD_DIM:HEAD_DIM + 8, :] * (1.0 / 8.0)
                pv = pv[:HEAD_DIM, :]
            else:
                sums = jnp.sum(p.reshape(kt // 8, 8, t), axis=0)
                pv = _dot(vt, p.astype(BF16))
            self.l[a] = alpha * self.l[a] + sums
            self.acc[a] = alpha[0:1, :] * self.acc[a] + pv
            self.m[a] = m_new

    def result(self):
        out = [self.acc[a] / jnp.sum(self.l[a], axis=0, keepdims=True) for a in range(2)]
        return jnp.concatenate(out, axis=0).T


def _flash_scratch(t, kt):
    return [pltpu.VMEM((4, t, LANES), BF16), pltpu.VMEM((4, kt, t), F32), pltpu.VMEM((4, 8, t), F32),
            pltpu.VMEM((2, 8, t), F32), pltpu.VMEM((2, 8, t), F32), pltpu.VMEM((2, HEAD_DIM, t), F32)]


def _split_heads(q_ref):
    q2 = q_ref[...]
    low = lax.broadcasted_iota(jnp.int32, q2.shape, 1) < HEAD_DIM
    moved = pltpu.roll(q2.astype(F32), HEAD_DIM, 1).astype(q2.dtype)
    return [jnp.where(low, q, jnp.zeros_like(q2)) for q in (q2, moved)]


def _diagonal_mask(kt, t, shift):
    key = lax.broadcasted_iota(jnp.int32, (kt, t), 0)
    query = lax.broadcasted_iota(jnp.int32, (kt, t), 1)
    return key <= query + shift


def _moba_kernel(q_ref, k_ref, vt_ref, o_ref, km32, km_hi, km_mid, km_lo, *flash_refs):
    t = q_ref.shape[0]
    nb = k_ref.shape[0] // MOBA_BLOCK
    qi = pl.program_id(1)
    fl = _Flash(flash_refs, mxu_sums=False)
    kt = fl.s.shape[1]
    n_tiles = k_ref.shape[0] // kt
    n_past = (qi * t) // kt

    @pl.when(qi == 0)
    def _():
        km32[...] = jnp.zeros_like(km32)

        def centroid(b, _):
            blk = k_ref[pl.ds(pl.multiple_of(b * MOBA_BLOCK, MOBA_BLOCK), MOBA_BLOCK), :].astype(F32)
            km32[pl.ds(HEAD_DIM + b, 1), :] = jnp.sum(blk, axis=0, keepdims=True) * (1.0 / MOBA_BLOCK)
            return 0
        lax.fori_loop(0, nb, centroid, 0)
        hi, mid, lo = _split3(km32[...])
        km_hi[...] = hi
        km_mid[...] = mid
        km_lo[...] = lo

    lane = lax.broadcasted_iota(jnp.int32, (t, LANES), 1)
    low = lane < HEAD_DIM
    blk = lane - HEAD_DIM
    own = (qi * t + lax.broadcasted_iota(jnp.int32, (t, LANES), 0)) // MOBA_BLOCK
    slot = lax.broadcasted_iota(jnp.int32, (LANES, t), 0)
    slot_f = slot.astype(F32)
    own_t = (qi * t + lax.broadcasted_iota(jnp.int32, (LANES, t), 1)) // MOBA_BLOCK
    candidate = jnp.logical_and(slot >= HEAD_DIM, slot - HEAD_DIM < own_t)
    for a, q in enumerate(_split_heads(q_ref)):
        cols = slice(a * LANES, (a + 1) * LANES)
        gate = _dot_nt(km_hi[:, cols], q) + _dot_nt(km_mid[:, cols], q) + _dot_nt(km_lo[:, cols], q)
        g = jnp.where(candidate, gate, NEG_INF)
        sel_t = jnp.zeros((LANES, t), F32)
        for rank in range(MOBA_TOPK):
            best = jnp.max(g, axis=0, keepdims=True)
            idx = jnp.min(jnp.where(g == best, slot_f, float(LANES)), axis=0, keepdims=True)
            pick = slot_f == idx
            sel_t = jnp.where(pick, jnp.where(rank < own_t, 1.0, 0.0), sel_t)
            g = jnp.where(pick, KNOCKOUT, g)
        sel = sel_t.T
        past_bias = jnp.where(sel > 0.0, 0.0, NEG_INF).astype(q.dtype)
        own_bias = jnp.where(jnp.logical_or(sel > 0.0, blk == own), 0.0, NEG_INF).astype(q.dtype)
        fl.qs[a] = jnp.where(low, q, past_bias)
        fl.qs[2 + a] = jnp.where(low, q, own_bias)

    def rows(j):
        return pl.ds(pl.multiple_of(jnp.minimum(j, n_tiles - 1) * kt, kt), kt)

    fl.init()
    fl.logits(0, 2, k_ref, rows(n_past), _diagonal_mask(kt, t, qi * t - n_past * kt))
    fl.logits(1, 0, k_ref, rows(0))
    fl.accumulate(0, vt_ref, rows(n_past))

    def body(i, _):
        fl.logits(0, 0, k_ref, rows(2 * i + 1))
        fl.accumulate(1, vt_ref, rows(2 * i))
        fl.logits(1, 0, k_ref, rows(2 * i + 2))
        fl.accumulate(0, vt_ref, rows(2 * i + 1))
        return 0

    lax.fori_loop(0, n_past // 2, body, 0)

    @pl.when(n_past % 2 == 1)
    def _():
        fl.accumulate(1, vt_ref, rows(n_past - 1))

    o_ref[...] = fl.result().astype(o_ref.dtype)


def _fox_kernel(cend_ref, q_ref, k_ref, vt_ref, o_ref, knorm_ref, *flash_refs):
    t = q_ref.shape[0]
    hp = pl.program_id(0)
    qi = pl.program_id(1)
    fl = _Flash(flash_refs, mxu_sums=True)
    kt = fl.s.shape[1]
    n_tiles = k_ref.shape[0] // kt
    n_past = (qi * t) // kt

    @pl.when(qi == 0)
    def _():
        key_lanes = lax.broadcasted_iota(jnp.int32, (1, 2 * LANES), 1) % LANES < HEAD_DIM

        def step(b, best):
            k = k_ref[pl.ds(pl.multiple_of(b * kt, kt), kt), :].astype(F32)
            sq = jnp.where(key_lanes, k * k, 0.0)
            return tuple(jnp.maximum(best[a], jnp.sum(sq[:, a * LANES:(a + 1) * LANES], axis=1, keepdims=True))
                         for a in range(2))
        zero = jnp.zeros((kt, 1), F32)
        best = lax.fori_loop(0, n_tiles, step, (zero, zero))
        for a in range(2):
            knorm_ref[a:a + 1, :] = jnp.broadcast_to(jnp.max(best[a], axis=0, keepdims=True), (1, LANES))

    lane = lax.broadcasted_iota(jnp.int32, (t, LANES), 1)
    heads = _split_heads(q_ref)
    for a in range(2):
        piece_lane = lane - (HEAD_DIM + 2 * hp + a)
        on_piece = jnp.logical_or(piece_lane == 0, jnp.logical_or(piece_lane == GATE_LANES,
                                                                 piece_lane == 2 * GATE_LANES))
        minus_one = jnp.where(on_piece, -1.0, 0.0).astype(q_ref.dtype)
        fl.qs[a] = jnp.where(lane < HEAD_DIM, heads[a], minus_one)

    def rows(j):
        return pl.ds(pl.multiple_of(jnp.maximum(j, 0) * kt, kt), kt)

    fl.init()
    fl.logits(0, 0, k_ref, rows(n_past), _diagonal_mask(kt, t, qi * t - n_past * kt))
    fl.logits(1, 0, k_ref, rows(n_past - 1))
    fl.accumulate(0, vt_ref, rows(n_past))

    reach = []
    for a in range(2):
        q32 = heads[a].astype(F32).T
        qn2 = jnp.sum(q32 * q32, axis=0, keepdims=True)
        zmax = jnp.sqrt(qn2 * knorm_ref[a:a + 1, 0:1]) * 1.001
        reach.append(jnp.max(zmax - fl.m[a][0:1, :]))

    def alive(j):
        jj = jnp.maximum(j, 0)
        live = [reach[a] - cend_ref[(2 * hp + a) * n_tiles + jj] > (EXP_ZERO - BOUND_SLACK) * LOG2E
                for a in range(2)]
        return jnp.logical_and(j >= 0, jnp.logical_or(live[0], live[1]))

    def body(j):
        fl.logits(0, 0, k_ref, rows(j - 1))
        fl.accumulate(1, vt_ref, rows(j))
        fl.logits(1, 0, k_ref, rows(j - 2))
        fl.accumulate(0, vt_ref, rows(j - 1))
        return j - 2

    j = lax.while_loop(lambda j: alive(j - 1), body, n_past - 1)

    @pl.when(alive(j))
    def _():
        fl.accumulate(1, vt_ref, rows(j))

    o_ref[...] = fl.result().astype(o_ref.dtype)


def _sb_attention(qkv, q_block0, k_block0, v_block0, n_pairs):
    S = qkv.shape[0]
    t = SB_GROUP * ATT_TILE
    assert S % t == 0
    return pl.pallas_call(
        _sb_kernel, grid=(n_pairs, S // t),
        in_specs=[pl.BlockSpec((t, LANES), lambda p, i: (i, q_block0 + p)),
                  pl.BlockSpec((S, LANES), lambda p, i: (0, k_block0 + p)),
                  pl.BlockSpec((S, LANES), lambda p, i: (0, v_block0 + p))],
        out_specs=pl.BlockSpec((t, LANES), lambda p, i: (i, p)),
        out_shape=jax.ShapeDtypeStruct((S, n_pairs * LANES), BF16),
        compiler_params=_params("arbitrary", "arbitrary"), name="sb",
    )(qkv, qkv, qkv)


def _flash_attention(kind, q_arr, q_block0, keys, values_t, n_pairs, c_end=None):
    S = q_arr.shape[0]
    t = min(FLASH_TILE[kind], S)
    kt = min(KEY_TILE[kind], S)
    assert S % kt == 0 and kt % t == 0 and t % MOBA_BLOCK == 0
    in_specs = [pl.BlockSpec((t, LANES), lambda p, i: (i, q_block0 + p)),
                pl.BlockSpec((S, 2 * LANES), lambda p, i: (0, p)),
                pl.BlockSpec((LANES, S), lambda p, i: (p, 0))]
    args = [q_arr, keys, values_t]
    if kind == "moba":
        body = _moba_kernel
        assert S // MOBA_BLOCK <= HEAD_DIM
        scratch = [pltpu.VMEM((LANES, 2 * LANES), F32)] + [pltpu.VMEM((LANES, 2 * LANES), BF16)] * 3
    else:
        body = _fox_kernel
        in_specs = [pl.BlockSpec(memory_space=pltpu.SMEM)] + in_specs
        args = [c_end] + args
        scratch = [pltpu.VMEM((8, LANES), F32)]
    return pl.pallas_call(
        body, grid=(n_pairs, S // t), in_specs=in_specs,
        out_specs=pl.BlockSpec((t, LANES), lambda p, i: (i, p)),
        out_shape=jax.ShapeDtypeStruct((S, n_pairs * LANES), BF16),
        scratch_shapes=scratch + _flash_scratch(t, kt),
        compiler_params=_params("arbitrary", "arbitrary"), name=kind,
    )(*args)


def _memkv_kernel(mem_ref, g_ref, w_ref, o_ref):
    o_ref[...] = _dot(_rms(mem_ref[...], g_ref[...]).astype(BF16), w_ref[...]).astype(o_ref.dtype)


def _memory_kv(mem, g, w_kv):
    M, D = mem.shape
    N = w_kv.shape[1]
    return pl.pallas_call(
        _memkv_kernel, out_shape=jax.ShapeDtypeStruct((M, N), BF16), name="memkv",
        compiler_params=pltpu.CompilerParams(vmem_limit_bytes=VMEM_LIMIT),
    )(mem, g.reshape(1, D), w_kv.astype(BF16))


def _post_kernel(*refs, n_att):
    att_refs = refs[:n_att]
    h_ref, wo_ref, g_ref, wq_ref, kv_ref, wxo_ref, o_ref = refs[n_att:]
    tm = h_ref.shape[0]
    xa_dim = wq_ref.shape[1]
    hd = xa_dim // XA_HEADS
    h1 = h_ref[...]
    col = 0
    for a_ref in att_refs:
        w = a_ref.shape[1]
        h1 = h1 + _dot(a_ref[...], wo_ref[col:col + w, :])
        col += w
    q = _dot(_rms(h1, g_ref[...]).astype(BF16), wq_ref[...]).astype(BF16)
    kv = kv_ref[...]
    k = kv[:, :xa_dim]
    v = kv[:, xa_dim:]
    lane = lax.broadcasted_iota(jnp.int32, (tm, xa_dim), 1)
    o = jnp.zeros((tm, xa_dim), F32)
    for hh in range(XA_HEADS):
        mine = (lane >= hh * hd) & (lane < (hh + 1) * hd)
        s = _dot_nt(jnp.where(mine, q, jnp.zeros_like(q)), k) * (hd ** -0.5)
        m = jnp.max(s, axis=1, keepdims=True)
        p = jnp.exp(s - m)
        l = jnp.sum(p, axis=1, keepdims=True)
        o = jnp.where(mine, _dot(p.astype(BF16), v) / l, o)
    o_ref[...] = h1 + _dot(o.astype(BF16), wxo_ref[...])


def _post(att_parts, h, w_out, g_xa, w_q, kv, w_xo):
    S, D = h.shape
    tm = min(ROW_TILE, S)
    row = lambda i: (i, 0)
    in_specs = [pl.BlockSpec((tm, a.shape[1]), row) for a in att_parts]
    in_specs += [pl.BlockSpec((tm, D), row), _resident(w_out.shape), _resident((1, D)),
                 _resident(w_q.shape), _resident(kv.shape), _resident(w_xo.shape)]
    return pl.pallas_call(
        functools.partial(_post_kernel, n_att=len(att_parts)),
        grid=(S // tm,), in_specs=in_specs, out_specs=pl.BlockSpec((tm, D), row),
        out_shape=jax.ShapeDtypeStruct((S, D), F32), compiler_params=_params("arbitrary"), name="post",
    )(*att_parts, h, w_out.astype(BF16), g_xa.reshape(1, D), w_q.astype(BF16), kv, w_xo.astype(BF16))


def _ffn_kernel(*refs, d_ff, final):
    if final:
        h_ref, g_ref, wup_ref, cw_ref, cb_ref, wdn_ref, gfin_ref, o_ref, prev_ref = refs
    else:
        h_ref, g_ref, wup_ref, cw_ref, cb_ref, wdn_ref, o_ref, prev_ref = refs
    tm = h_ref.shape[0]

    @pl.when(pl.program_id(0) == 0)
    def _():
        prev_ref[...] = jnp.zeros_like(prev_ref)

    h = h_ref[...]
    hn = _rms(h, g_ref[...]).astype(BF16)
    rowi = lax.broadcasted_iota(jnp.int32, (tm, FF_CHUNK), 0)

    def up(c):
        return [_dot(hn, wup_ref[:, col0:col0 + FF_CHUNK]) for col0 in (c * FF_CHUNK, d_ff + c * FF_CHUNK)]

    def conv(col0, u):
        cols = slice(col0, col0 + FF_CHUNK)
        prev = prev_ref[:, cols]
        p1, p2 = prev[7:8, :], prev[6:7, :]
        u1 = jnp.where(rowi == 0, p1, pltpu.roll(u, 1, 0))
        u2 = jnp.where(rowi == 0, p2, jnp.where(rowi == 1, p1, pltpu.roll(u, 2, 0)))
        prev_ref[:, cols] = u[tm - 8:tm, :]
        w = cw_ref[:, cols]
        return cb_ref[:, cols] + u2 * w[0:1, :] + u1 * w[1:2, :] + u * w[2:3, :]

    n_chunks = d_ff // FF_CHUNK
    u_next = up(0)
    down = None
    for c in range(n_chunks):
        u_gate, u_val = u_next
        if c + 1 < n_chunks:
            u_next = up(c + 1)
        gate = conv(c * FF_CHUNK, u_gate)
        val = conv(d_ff + c * FF_CHUNK, u_val)
        act = gate * (1.0 / (1.0 + jnp.exp(-gate))) * val
        part = _dot(act.astype(BF16), wdn_ref[c * FF_CHUNK:(c + 1) * FF_CHUNK, :])
        down = part if down is None else down + part
    acc = h + down
    if final:
        acc = _rms(acc, gfin_ref[...])
    o_ref[...] = acc


def _ffn(h, g, w_up, conv_w, conv_b, w_down, g_final=None):
    S, D = h.shape
    d_ff = w_down.shape[0]
    tm = min(ROW_TILE, S)
    assert d_ff % FF_CHUNK == 0 and conv_w.shape[0] == CONV_WIDTH
    row = lambda i: (i, 0)
    in_specs = [pl.BlockSpec((tm, D), row), _resident((1, D)), _resident(w_up.shape),
                _resident(conv_w.shape), _resident((1, 2 * d_ff)), _resident(w_down.shape)]
    args = [h, g.reshape(1, D), w_up.astype(BF16), conv_w, conv_b.reshape(1, 2 * d_ff), w_down.astype(BF16)]
    if g_final is not None:
        in_specs.append(_resident((1, D)))
        args.append(g_final.reshape(1, D))
    return pl.pallas_call(
        functools.partial(_ffn_kernel, d_ff=d_ff, final=g_final is not None),
        grid=(S // tm,), in_specs=in_specs, out_specs=pl.BlockSpec((tm, D), row),
        out_shape=jax.ShapeDtypeStruct((S, D), F32),
        scratch_shapes=[pltpu.VMEM((8, 2 * d_ff), F32)],
        compiler_params=_params("arbitrary"), name="ffn",
    )(*args)


def kernel(x, mem, positions, norm_mix_g, norm_xa_g, norm_mem_g, norm_ffn_g, ab_w_in, ab_w_out, fox_w_in, fox_b_f, fox_w_out, xa_w_q, xa_w_kv, xa_w_out, ffn_w_up, ffn_conv_w, ffn_conv_b, ffn_w_down, final_norm_g):
    B, S, D = x.shape
    depth = norm_mix_g.shape[0]
    n_pairs = D // LANES
    q_scale2 = HEAD_DIM ** -0.5 * LOG2E
    outs = []
    for b in range(B):
        h = x[b]
        for layer in range(depth):
            n = layer // 2
            if layer % 2 == 0:
                assert D == 2 * PROJ_CHUNK
                half = n_pairs // 2
                w_in = ab_w_in[n]
                qkv, k_moba, vt_moba = _project(
                    h, norm_mix_g[layer], w_in, w_in[:, 2 * D + PROJ_CHUNK:].T,
                    (1.0, q_scale2, 1.0, 1.0, 1.0, 1.0), range(6), (3,), rope=(positions[b], (1, 3)))
                att = [_sb_attention(qkv, 0, n_pairs, 2 * n_pairs, half),
                       _flash_attention("moba", qkv, half, k_moba, vt_moba, half)]
                w_out = ab_w_out[n]
            else:
                w_in = fox_w_in[n]
                q, keys, vt, c2 = _project(
                    h, norm_mix_g[layer], w_in[:, :2 * D], w_in[:, 2 * D:3 * D].T,
                    (q_scale2, q_scale2, 1.0, 1.0), (0, 1), (2, 3), gates=(w_in[:, 3 * D:], fox_b_f[n]))
                kt = min(KEY_TILE["fox"], S)
                c_end = c2[:, kt - 1::kt].reshape(-1)
                att = [_flash_attention("fox", q, 0, keys, vt, n_pairs, c_end)]
                w_out = fox_w_out[n]
            kv = _memory_kv(mem[b], norm_mem_g[layer], xa_w_kv[layer])
            h = _post(att, h, w_out, norm_xa_g[layer], xa_w_q[layer], kv, xa_w_out[layer])
            h = _ffn(h, norm_ffn_g[layer], ffn_w_up[layer], ffn_conv_w[layer], ffn_conv_b[layer],
                     ffn_w_down[layer], final_norm_g if layer == depth - 1 else None)
        outs.append(h)
    return jnp.stack(outs)
```

```python
import functools

import numpy as np
import jax
import jax.numpy as jnp
from jax import lax
from jax.experimental import pallas as pl
from jax.experimental.pallas import tpu as pltpu

F32 = jnp.float32
BF16 = jnp.bfloat16

HEAD_DIM = 64
LANES = 128
MOBA_BLOCK = 256
MOBA_TOPK = 3
ROPE_THETA = 10000.0
XA_HEADS = 4
CONV_WIDTH = 3
RMS_EPS = 1e-6
NEG_INF = -1e9
KNOCKOUT = -3.0e38
LOG2E = float(np.log2(np.e))
EXP_ZERO = -104.0
BOUND_SLACK = 1.0
ATT_TILE = 256
SB_GROUP = 4
ONES_ROWS = 16
GATE_LANES = 16
FLASH_TILE = {"moba": 1024, "fox": 512}
KEY_TILE = {"moba": 1024, "fox": 512}
ROW_TILE = 1024
PROJ_CHUNK = 512
FF_CHUNK = 256
VMEM_LIMIT = 56 * 1024 * 1024

_NT = (((1,), (1,)), ((), ()))


def _dot(a, b):
    return jnp.dot(a, b, preferred_element_type=F32)


def _dot_nt(a, b):
    return lax.dot_general(a, b, _NT, preferred_element_type=F32)


def _split2(x):
    hi = x.astype(BF16)
    lo = (x - hi.astype(F32)).astype(BF16)
    return hi, lo


def _split3(x):
    hi = x.astype(BF16)
    r = x - hi.astype(F32)
    mid = r.astype(BF16)
    lo = (r - mid.astype(F32)).astype(BF16)
    return hi, mid, lo


def _rms(x, g):
    return x * lax.rsqrt(jnp.mean(x * x, axis=-1, keepdims=True) + RMS_EPS) * g


def _log_sigmoid(z):
    return jnp.minimum(z, 0.0) - jnp.log(1.0 + jnp.exp(-jnp.abs(z)))


def _resident(shape):
    return pl.BlockSpec(shape, lambda *_: (0,) * len(shape), pipeline_mode=pl.Buffered(1))


def _params(*sem):
    return pltpu.CompilerParams(dimension_semantics=sem, vmem_limit_bytes=VMEM_LIMIT)


def _proj_kernel(*refs, chunk_scale, rope_chunks, out_chunks, key_chunks, gate_keys, n_vt):
    refs = list(refs)
    x_ref, g_ref, w_ref = refs[:3]
    del refs[:3]
    pos_ref = refs.pop(0) if rope_chunks else None
    wvt_ref = refs.pop(0)
    if gate_keys:
        wft_ref, bf_ref = refs[:2]
        del refs[:2]
    o_ref, kaug_ref, vt_ref = refs[:3]
    del refs[:3]
    if gate_keys:
        crow_ref, carry_ref = refs
    tm = x_ref.shape[0]
    hn32 = _rms(x_ref[...], g_ref[...])
    hn = hn32.astype(BF16)
    lane = lax.broadcasted_iota(jnp.int32, (tm, LANES), 1)
    low = lane < HEAD_DIM

    if rope_chunks:
        half = HEAD_DIM // 2
        first_half = (lane % HEAD_DIM) < half
        freq_id = (lax.broadcasted_iota(jnp.int32, (1, LANES), 1) % half).astype(F32)
        inv_freq = jnp.power(ROPE_THETA, -freq_id / half)
        ang = pos_ref[...].astype(F32) * inv_freq
        cos, sin = jnp.cos(ang), jnp.sin(ang)
        sin_signed = jnp.where(first_half, -sin, sin)

    if gate_keys:
        @pl.when(pl.program_id(0) == 0)
        def _():
            carry_ref[...] = jnp.zeros_like(carry_ref)

        h_hi, h_lo = _split2(hn32)
        w_hi, w_lo = _split2(wft_ref[...])
        f = _dot_nt(w_hi, h_hi) + _dot_nt(w_hi, h_lo) + _dot_nt(w_lo, h_hi) + bf_ref[...]
        r = lax.broadcasted_iota(jnp.int32, (tm, tm), 0)
        cidx = lax.broadcasted_iota(jnp.int32, (tm, tm), 1)
        incl = (r <= cidx).astype(BF16)
        c_row = carry_ref[...]
        for piece in _split3(_log_sigmoid(f)):
            c_row = c_row + _dot(piece, incl)
        carry_ref[...] = c_row[:, tm - 1:tm]
        c2 = c_row * LOG2E
        crow_ref[...] = c2
        n_heads = c2.shape[0]
        c2_col = jnp.concatenate([c2, jnp.zeros((LANES - n_heads, tm), F32)], axis=0).T
        extras = sum(pltpu.roll(piece.astype(F32), HEAD_DIM + GATE_LANES * i, 1)
                     for i, piece in enumerate(_split3(c2_col)))
    elif key_chunks:
        row = pl.program_id(0) * tm + lax.broadcasted_iota(jnp.int32, (tm, LANES), 0)
        one_hot = (lane - HEAD_DIM == row // MOBA_BLOCK).astype(F32)

    for c in range(len(chunk_scale)):
        y = _dot(hn, w_ref[:, c * PROJ_CHUNK:(c + 1) * PROJ_CHUNK])
        parts = [y[:, s * LANES:(s + 1) * LANES] for s in range(PROJ_CHUNK // LANES)]
        if c in rope_chunks:
            parts = [ys * cos + sin_signed * jnp.where(first_half, pltpu.roll(ys, LANES - HEAD_DIM // 2, 1),
                                                       pltpu.roll(ys, HEAD_DIM // 2, 1)) for ys in parts]
        if c in key_chunks:
            base = key_chunks.index(c) * 2 * len(parts)
            for s, ys in enumerate(parts):
                for a, head in enumerate((ys, pltpu.roll(ys, HEAD_DIM, 1))):
                    col = (base + 2 * s + a) * LANES
                    upper = extras if gate_keys else one_hot
                    kaug_ref[:, col:col + LANES] = jnp.where(low, head, upper).astype(BF16)
        if c in out_chunks:
            y = jnp.concatenate(parts, axis=1)
            if chunk_scale[c] != 1.0:
                y = y * chunk_scale[c]
            col = out_chunks.index(c) * PROJ_CHUNK
            o_ref[:, col:col + PROJ_CHUNK] = y.astype(o_ref.dtype)

    for i in range(n_vt // PROJ_CHUNK):
        rows = slice(i * PROJ_CHUNK, (i + 1) * PROJ_CHUNK)
        vt_ref[rows, :] = _dot_nt(wvt_ref[rows, :], hn).astype(BF16)


def _project(x, g, w, w_vt, chunk_scale, out_chunks, key_chunks, *, rope=None, gates=None):
    S, D = x.shape
    tm = min(ROW_TILE, S)
    n_vt = w_vt.shape[0]
    n_key_lanes = len(key_chunks) * 2 * PROJ_CHUNK
    assert S % tm == 0 and w.shape[1] == PROJ_CHUNK * len(chunk_scale) and n_vt % PROJ_CHUNK == 0
    row = lambda i: (i, 0)
    in_specs = [pl.BlockSpec((tm, D), row), _resident((1, D)), _resident(w.shape)]
    args = [x, g.reshape(1, D), w.astype(BF16)]
    rope_chunks = ()
    if rope is not None:
        positions, rope_chunks = rope
        assert S // MOBA_BLOCK <= HEAD_DIM
        in_specs += [pl.BlockSpec((tm, 1), row)]
        args += [positions.reshape(S, 1)]
    in_specs += [_resident(w_vt.shape)]
    args += [w_vt.astype(BF16)]
    out_shape = [jax.ShapeDtypeStruct((S, len(out_chunks) * PROJ_CHUNK), BF16),
                 jax.ShapeDtypeStruct((S, n_key_lanes), BF16), jax.ShapeDtypeStruct((n_vt, S), BF16)]
    out_specs = [pl.BlockSpec((tm, len(out_chunks) * PROJ_CHUNK), row), pl.BlockSpec((tm, n_key_lanes), row),
                 pl.BlockSpec((n_vt, tm), lambda i: (0, i))]
    scratch = []
    if gates is not None:
        w_f, b_f = gates
        H = w_f.shape[1]
        assert H * LANES == n_key_lanes and H <= GATE_LANES
        in_specs += [_resident((H, D)), _resident((H, 1))]
        args += [w_f.T, b_f.reshape(H, 1)]
        out_shape += [jax.ShapeDtypeStruct((H, S), F32)]
        out_specs += [pl.BlockSpec((H, tm), lambda i: (0, i))]
        scratch = [pltpu.VMEM((H, 1), F32)]
    return pl.pallas_call(
        functools.partial(_proj_kernel, chunk_scale=tuple(chunk_scale), rope_chunks=tuple(rope_chunks),
                          out_chunks=tuple(out_chunks), key_chunks=tuple(key_chunks),
                          gate_keys=gates is not None, n_vt=n_vt),
        grid=(S // tm,), in_specs=in_specs, out_specs=out_specs, out_shape=out_shape,
        scratch_shapes=scratch, compiler_params=_params("arbitrary"),
        name="proj_gates" if gates is not None else "proj_rope",
    )(*args)


def _head_lanes(shape, a):
    lane = lax.broadcasted_iota(jnp.int32, shape, len(shape) - 1)
    return (lane >= a * HEAD_DIM) & (lane < (a + 1) * HEAD_DIM)


def _kv_block(k_ref, v_ref, j, t):
    rows = pl.ds(pl.multiple_of(j * t, t), t)
    return k_ref[rows, :], v_ref[rows, :]


def _sb_kernel(q_ref, k_ref, v_ref, o_ref):
    t = ATT_TILE
    n_sub = q_ref.shape[0] // t
    step = pl.program_id(1)
    scale = HEAD_DIM ** -0.5
    r = lax.broadcasted_iota(jnp.int32, (t, t), 0)
    c = lax.broadcasted_iota(jnp.int32, (t, t), 1)
    strict = c < r
    later = (r > c).astype(BF16)
    q2 = q_ref[...] * jnp.asarray(scale, q_ref.dtype)
    lanes = [_head_lanes((t, LANES), a) for a in range(2)]
    chains = [(sub, a) for sub in range(n_sub) for a in range(2)]
    qs = [jnp.where(lanes[a], q2[sub * t:(sub + 1) * t, :], jnp.zeros((t, LANES), q2.dtype)) for sub, a in chains]

    def inner(qs, ks, diagonal):
        n = range(len(qs))
        z = [_dot_nt(qs[i], ks[i]) for i in n]
        lb = [_log_sigmoid(z[i]) for i in n]
        l1 = [lb[i] - z[i] for i in n]
        if diagonal:
            l1 = [jnp.where(strict, l1[i], 0.0) for i in n]
        parts = [_split2(l1[i]) for i in n]
        tail = [_dot(parts[i][0], later) + _dot(parts[i][1], later) for i in n]
        u = [lb[i] + tail[i] for i in n]
        if diagonal:
            u = [jnp.where(strict, u[i], NEG_INF) for i in n]
        return u, [tail[i][:, 0:1] + l1[i][:, 0:1] for i in n]

    def weigh(u, carry, vs):
        return [_dot(jnp.exp(u[i] + carry[i]).astype(BF16), vs[i]) for i in range(len(u))]

    tiles = [step * n_sub + sub for sub in range(n_sub)]
    own = [_kv_block(k_ref, v_ref, qi, t) for qi in tiles]
    prev = [_kv_block(k_ref, v_ref, jnp.maximum(qi - 1, 0), t) for qi in tiles]
    prev = [(k, v * jnp.where(qi > 0, 1.0, 0.0).astype(v.dtype)) for (k, v), qi in zip(prev, tiles)]
    u_own, sum_own = inner(qs, [own[sub][0] for sub, _ in chains], True)
    u_prev, sum_prev = inner(qs, [prev[sub][0] for sub, _ in chains], False)
    acc = weigh(u_own, [0.0] * len(chains), [own[sub][1] for sub, _ in chains])
    pv = weigh(u_prev, sum_own, [prev[sub][1] for sub, _ in chains])
    acc = [acc[i] + pv[i] for i in range(len(chains))]
    carry = [sum_own[i] + sum_prev[i] for i in range(len(chains))]

    for sub in range(n_sub):
        i0, i1 = 2 * sub, 2 * sub + 1

        def cond(st):
            kb, carry0, carry1, _, _ = st
            return jnp.logical_and(kb >= 0, jnp.max(jnp.maximum(carry0, carry1)) > EXP_ZERO)

        def body(st, i0=i0, i1=i1):
            kb, carry0, carry1, acc0, acc1 = st
            k, v = _kv_block(k_ref, v_ref, kb, t)
            u, sums = inner([qs[i0], qs[i1]], [k, k], False)
            pv = weigh(u, [carry0, carry1], [v, v])
            return kb - 1, carry0 + sums[0], carry1 + sums[1], acc0 + pv[0], acc1 + pv[1]

        _, _, _, acc0, acc1 = lax.while_loop(cond, body, (tiles[sub] - 2, carry[i0], carry[i1], acc[i0], acc[i1]))
        o_ref[sub * t:(sub + 1) * t, :] = jnp.where(lanes[0], acc0, acc1).astype(o_ref.dtype)


class _Flash:
    def __init__(self, refs, mxu_sums):
        self.qs, self.s, self.cm, self.m, self.l, self.acc = refs
        self.mxu_sums = mxu_sums

    def init(self):
        for a in range(2):
            self.m[a] = jnp.full(self.m.shape[1:], KNOCKOUT, F32)
            self.l[a] = jnp.zeros(self.l.shape[1:], F32)
            self.acc[a] = jnp.zeros(self.acc.shape[1:], F32)

    def logits(self, buf, q_slot0, k_ref, rows, mask=None):
        for a in range(2):
            s = _dot_nt(k_ref[rows, a * LANES:(a + 1) * LANES], self.qs[q_slot0 + a])
            if mask is not None:
                s = jnp.where(mask, s, NEG_INF)
            self.s[2 * buf + a] = s
            kt, t = s.shape
            top = jnp.max(jnp.max(s.reshape(kt // 8, 8, t), axis=0), axis=0, keepdims=True)
            self.cm[2 * buf + a] = jnp.broadcast_to(top, (8, t))

    def accumulate(self, buf, vt_ref, rows):
        for a in range(2):
            m_new = jnp.maximum(self.m[a], self.cm[2 * buf + a])
            alpha = jnp.exp2(self.m[a] - m_new)
            s = self.s[2 * buf + a]
            kt, t = s.shape
            p = jnp.exp2(s - m_new[0:1, :])
            vt = vt_ref[a * HEAD_DIM:(a + 1) * HEAD_DIM, rows]
            if self.mxu_sums:
                pv = _dot(jnp.concatenate([vt, jnp.ones((ONES_ROWS, kt), BF16)], axis=0), p.astype(BF16))
                sums = pv[HEAD_DIM:HEAD_DIM + 8, :] * (1.0 / 8.0)
                pv = pv[:HEAD_DIM, :]
            else:
                sums = jnp.sum(p.reshape(kt // 8, 8, t), axis=0)
                pv = _dot(vt, p.astype(BF16))
            self.l[a] = alpha * self.l[a] + sums
            self.acc[a] = alpha[0:1, :] * self.acc[a] + pv
            self.m[a] = m_new

    def result(self):
        out = [self.acc[a] / jnp.sum(self.l[a], axis=0, keepdims=True) for a in range(2)]
        return jnp.concatenate(out, axis=0).T


def _flash_scratch(t, kt):
    return [pltpu.VMEM((4, t, LANES), BF16), pltpu.VMEM((4, kt, t), F32), pltpu.VMEM((4, 8, t), F32),
            pltpu.VMEM((2, 8, t), F32), pltpu.VMEM((2, 8, t), F32), pltpu.VMEM((2, HEAD_DIM, t), F32)]


def _split_heads(q_ref):
    q2 = q_ref[...]
    low = lax.broadcasted_iota(jnp.int32, q2.shape, 1) < HEAD_DIM
    moved = pltpu.roll(q2.astype(F32), HEAD_DIM, 1).astype(q2.dtype)
    return [jnp.where(low, q, jnp.zeros_like(q2)) for q in (q2, moved)]


def _diagonal_mask(kt, t, shift):
    key = lax.broadcasted_iota(jnp.int32, (kt, t), 0)
    query = lax.broadcasted_iota(jnp.int32, (kt, t), 1)
    return key <= query + shift


def _moba_kernel(q_ref, k_ref, vt_ref, o_ref, km32, km_hi, km_mid, km_lo, *flash_refs):
    t = q_ref.shape[0]
    nb = k_ref.shape[0] // MOBA_BLOCK
    qi = pl.program_id(1)
    fl = _Flash(flash_refs, mxu_sums=False)
    kt = fl.s.shape[1]
    n_tiles = k_ref.shape[0] // kt
    n_past = (qi * t) // kt

    @pl.when(qi == 0)
    def _():
        km32[...] = jnp.zeros_like(km32)

        def centroid(b, _):
            blk = k_ref[pl.ds(pl.multiple_of(b * MOBA_BLOCK, MOBA_BLOCK), MOBA_BLOCK), :].astype(F32)
            km32[pl.ds(HEAD_DIM + b, 1), :] = jnp.sum(blk, axis=0, keepdims=True) * (1.0 / MOBA_BLOCK)
            return 0
        lax.fori_loop(0, nb, centroid, 0)
        hi, mid, lo = _split3(km32[...])
        km_hi[...] = hi
        km_mid[...] = mid
        km_lo[...] = lo

    lane = lax.broadcasted_iota(jnp.int32, (t, LANES), 1)
    low = lane < HEAD_DIM
    blk = lane - HEAD_DIM
    own = (qi * t + lax.broadcasted_iota(jnp.int32, (t, LANES), 0)) // MOBA_BLOCK
    slot = lax.broadcasted_iota(jnp.int32, (LANES, t), 0)
    slot_f = slot.astype(F32)
    own_t = (qi * t + lax.broadcasted_iota(jnp.int32, (LANES, t), 1)) // MOBA_BLOCK
    candidate = jnp.logical_and(slot >= HEAD_DIM, slot - HEAD_DIM < own_t)
    for a, q in enumerate(_split_heads(q_ref)):
        cols = slice(a * LANES, (a + 1) * LANES)
        gate = _dot_nt(km_hi[:, cols], q) + _dot_nt(km_mid[:, cols], q) + _dot_nt(km_lo[:, cols], q)
        g = jnp.where(candidate, gate, NEG_INF)
        sel_t = jnp.zeros((LANES, t), F32)
        for rank in range(MOBA_TOPK):
            best = jnp.max(g, axis=0, keepdims=True)
            idx = jnp.min(jnp.where(g == best, slot_f, float(LANES)), axis=0, keepdims=True)
            pick = slot_f == idx
            sel_t = jnp.where(pick, jnp.where(rank < own_t, 1.0, 0.0), sel_t)
            g = jnp.where(pick, KNOCKOUT, g)
        sel = sel_t.T
        past_bias = jnp.where(sel > 0.0, 0.0, NEG_INF).astype(q.dtype)
        own_bias = jnp.where(jnp.logical_or(sel > 0.0, blk == own), 0.0, NEG_INF).astype(q.dtype)
        fl.qs[a] = jnp.where(low, q, past_bias)
        fl.qs[2 + a] = jnp.where(low, q, own_bias)

    def rows(j):
        return pl.ds(pl.multiple_of(jnp.minimum(j, n_tiles - 1) * kt, kt), kt)

    fl.init()
    fl.logits(0, 2, k_ref, rows(n_past), _diagonal_mask(kt, t, qi * t - n_past * kt))
    fl.logits(1, 0, k_ref, rows(0))
    fl.accumulate(0, vt_ref, rows(n_past))

    def body(i, _):
        fl.logits(0, 0, k_ref, rows(2 * i + 1))
        fl.accumulate(1, vt_ref, rows(2 * i))
        fl.logits(1, 0, k_ref, rows(2 * i + 2))
        fl.accumulate(0, vt_ref, rows(2 * i + 1))
        return 0

    lax.fori_loop(0, n_past // 2, body, 0)

    @pl.when(n_past % 2 == 1)
    def _():
        fl.accumulate(1, vt_ref, rows(n_past - 1))

    o_ref[...] = fl.result().astype(o_ref.dtype)


def _fox_kernel(cend_ref, q_ref, k_ref, vt_ref, o_ref, knorm_ref, *flash_refs):
    t = q_ref.shape[0]
    hp = pl.program_id(0)
    qi = pl.program_id(1)
    fl = _Flash(flash_refs, mxu_sums=True)
    kt = fl.s.shape[1]
    n_tiles = k_ref.shape[0] // kt
    n_past = (qi * t) // kt

    @pl.when(qi == 0)
    def _():
        key_lanes = lax.broadcasted_iota(jnp.int32, (1, 2 * LANES), 1) % LANES < HEAD_DIM

        def step(b, best):
            k = k_ref[pl.ds(pl.multiple_of(b * kt, kt), kt), :].astype(F32)
            sq = jnp.where(key_lanes, k * k, 0.0)
            return tuple(jnp.maximum(best[a], jnp.sum(sq[:, a * LANES:(a + 1) * LANES], axis=1, keepdims=True))
                         for a in range(2))
        zero = jnp.zeros((kt, 1), F32)
        best = lax.fori_loop(0, n_tiles, step, (zero, zero))
        for a in range(2):
            knorm_ref[a:a + 1, :] = jnp.broadcast_to(jnp.max(best[a], axis=0, keepdims=True), (1, LANES))

    lane = lax.broadcasted_iota(jnp.int32, (t, LANES), 1)
    heads = _split_heads(q_ref)
    for a in range(2):
        piece_lane = lane - (HEAD_DIM + 2 * hp + a)
        on_piece = jnp.logical_or(piece_lane == 0, jnp.logical_or(piece_lane == GATE_LANES,
                                                                 piece_lane == 2 * GATE_LANES))
        minus_one = jnp.where(on_piece, -1.0, 0.0).astype(q_ref.dtype)
        fl.qs[a] = jnp.where(lane < HEAD_DIM, heads[a], minus_one)

    def rows(j):
        return pl.ds(pl.multiple_of(jnp.maximum(j, 0) * kt, kt), kt)

    fl.init()
    fl.logits(0, 0, k_ref, rows(n_past), _diagonal_mask(kt, t, qi * t - n_past * kt))
    fl.logits(1, 0, k_ref, rows(n_past - 1))
    fl.accumulate(0, vt_ref, rows(n_past))

    reach = []
    for a in range(2):
        q32 = heads[a].astype(F32).T
        qn2 = jnp.sum(q32 * q32, axis=0, keepdims=True)
        zmax = jnp.sqrt(qn2 * knorm_ref[a:a + 1, 0:1]) * 1.001
        reach.append(jnp.max(zmax - fl.m[a][0:1, :]))

    def alive(j):
        jj = jnp.maximum(j, 0)
        live = [reach[a] - cend_ref[(2 * hp + a) * n_tiles + jj] > (EXP_ZERO - BOUND_SLACK) * LOG2E
                for a in range(2)]
        return jnp.logical_and(j >= 0, jnp.logical_or(live[0], live[1]))

    def body(j):
        fl.logits(0, 0, k_ref, rows(j - 1))
        fl.accumulate(1, vt_ref, rows(j))
        fl.logits(1, 0, k_ref, rows(j - 2))
        fl.accumulate(0, vt_ref, rows(j - 1))
        return j - 2

    j = lax.while_loop(lambda j: alive(j - 1), body, n_past - 1)

    @pl.when(alive(j))
    def _():
        fl.accumulate(1, vt_ref, rows(j))

    o_ref[...] = fl.result().astype(o_ref.dtype)


def _sb_attention(qkv, q_block0, k_block0, v_block0, n_pairs):
    S = qkv.shape[0]
    t = SB_GROUP * ATT_TILE
    assert S % t == 0
    return pl.pallas_call(
        _sb_kernel, grid=(n_pairs, S // t),
        in_specs=[pl.BlockSpec((t, LANES), lambda p, i: (i, q_block0 + p)),
                  pl.BlockSpec((S, LANES), lambda p, i: (0, k_block0 + p)),
                  pl.BlockSpec((S, LANES), lambda p, i: (0, v_block0 + p))],
        out_specs=pl.BlockSpec((t, LANES), lambda p, i: (i, p)),
        out_shape=jax.ShapeDtypeStruct((S, n_pairs * LANES), BF16),
        compiler_params=_params("arbitrary", "arbitrary"), name="sb",
    )(qkv, qkv, qkv)


def _flash_attention(kind, q_arr, q_block0, keys, values_t, n_pairs, c_end=None):
    S = q_arr.shape[0]
    t = min(FLASH_TILE[kind], S)
    kt = min(KEY_TILE[kind], S)
    assert S % kt == 0 and kt % t == 0 and t % MOBA_BLOCK == 0
    in_specs = [pl.BlockSpec((t, LANES), lambda p, i: (i, q_block0 + p)),
                pl.BlockSpec((S, 2 * LANES), lambda p, i: (0, p)),
                pl.BlockSpec((LANES, S), lambda p, i: (p, 0))]
    args = [q_arr, keys, values_t]
    if kind == "moba":
        body = _moba_kernel
        assert S // MOBA_BLOCK <= HEAD_DIM
        scratch = [pltpu.VMEM((LANES, 2 * LANES), F32)] + [pltpu.VMEM((LANES, 2 * LANES), BF16)] * 3
    else:
        body = _fox_kernel
        in_specs = [pl.BlockSpec(memory_space=pltpu.SMEM)] + in_specs
        args = [c_end] + args
        scratch = [pltpu.VMEM((8, LANES), F32)]
    return pl.pallas_call(
        body, grid=(n_pairs, S // t), in_specs=in_specs,
        out_specs=pl.BlockSpec((t, LANES), lambda p, i: (i, p)),
        out_shape=jax.ShapeDtypeStruct((S, n_pairs * LANES), BF16),
        scratch_shapes=scratch + _flash_scratch(t, kt),
        compiler_params=_params("arbitrary", "arbitrary"), name=kind,
    )(*args)


def _memkv_kernel(mem_ref, g_ref, w_ref, o_ref):
    o_ref[...] = _dot(_rms(mem_ref[...], g_ref[...]).astype(BF16), w_ref[...]).astype(o_ref.dtype)


def _memory_kv(mem, g, w_kv):
    M, D = mem.shape
    N = w_kv.shape[1]
    return pl.pallas_call(
        _memkv_kernel, out_shape=jax.ShapeDtypeStruct((M, N), BF16), name="memkv",
        compiler_params=pltpu.CompilerParams(vmem_limit_bytes=VMEM_LIMIT),
    )(mem, g.reshape(1, D), w_kv.astype(BF16))


def _post_kernel(*refs, n_att):
    att_refs = refs[:n_att]
    h_ref, wo_ref, g_ref, wq_ref, kv_ref, wxo_ref, o_ref = refs[n_att:]
    tm = h_ref.shape[0]
    xa_dim = wq_ref.shape[1]
    hd = xa_dim // XA_HEADS
    h1 = h_ref[...]
    col = 0
    for a_ref in att_refs:
        w = a_ref.shape[1]
        h1 = h1 + _dot(a_ref[...], wo_ref[col:col + w, :])
        col += w
    q = _dot(_rms(h1, g_ref[...]).astype(BF16), wq_ref[...]).astype(BF16)
    kv = kv_ref[...]
    k = kv[:, :xa_dim]
    v = kv[:, xa_dim:]
    lane = lax.broadcasted_iota(jnp.int32, (tm, xa_dim), 1)
    o = jnp.zeros((tm, xa_dim), F32)
    for hh in range(XA_HEADS):
        mine = (lane >= hh * hd) & (lane < (hh + 1) * hd)
        s = _dot_nt(jnp.where(mine, q, jnp.zeros_like(q)), k) * (hd ** -0.5)
        m = jnp.max(s, axis=1, keepdims=True)
        p = jnp.exp(s - m)
        l = jnp.sum(p, axis=1, keepdims=True)
        o = jnp.where(mine, _dot(p.astype(BF16), v) / l, o)
    o_ref[...] = h1 + _dot(o.astype(BF16), wxo_ref[...])


def _post(att_parts, h, w_out, g_xa, w_q, kv, w_xo):
    S, D = h.shape
    tm = min(ROW_TILE, S)
    row = lambda i: (i, 0)
    in_specs = [pl.BlockSpec((tm, a.shape[1]), row) for a in att_parts]
    in_specs += [pl.BlockSpec((tm, D), row), _resident(w_out.shape), _resident((1, D)),
                 _resident(w_q.shape), _resident(kv.shape), _resident(w_xo.shape)]
    return pl.pallas_call(
        functools.partial(_post_kernel, n_att=len(att_parts)),
        grid=(S // tm,), in_specs=in_specs, out_specs=pl.BlockSpec((tm, D), row),
        out_shape=jax.ShapeDtypeStruct((S, D), F32), compiler_params=_params("arbitrary"), name="post",
    )(*att_parts, h, w_out.astype(BF16), g_xa.reshape(1, D), w_q.astype(BF16), kv, w_xo.astype(BF16))


def _ffn_kernel(*refs, d_ff, final):
    if final:
        h_ref, g_ref, wup_ref, cw_ref, cb_ref, wdn_ref, gfin_ref, o_ref, prev_ref = refs
    else:
        h_ref, g_ref, wup_ref, cw_ref, cb_ref, wdn_ref, o_ref, prev_ref = refs
    tm = h_ref.shape[0]

    @pl.when(pl.program_id(0) == 0)
    def _():
        prev_ref[...] = jnp.zeros_like(prev_ref)

    h = h_ref[...]
    hn = _rms(h, g_ref[...]).astype(BF16)
    rowi = lax.broadcasted_iota(jnp.int32, (tm, FF_CHUNK), 0)

    def up(c):
        return [_dot(hn, wup_ref[:, col0:col0 + FF_CHUNK]) for col0 in (c * FF_CHUNK, d_ff + c * FF_CHUNK)]

    def conv(col0, u):
        cols = slice(col0, col0 + FF_CHUNK)
        prev = prev_ref[:, cols]
        p1, p2 = prev[7:8, :], prev[6:7, :]
        u1 = jnp.where(rowi == 0, p1, pltpu.roll(u, 1, 0))
        u2 = jnp.where(rowi == 0, p2, jnp.where(rowi == 1, p1, pltpu.roll(u, 2, 0)))
        prev_ref[:, cols] = u[tm - 8:tm, :]
        w = cw_ref[:, cols]
        return cb_ref[:, cols] + u2 * w[0:1, :] + u1 * w[1:2, :] + u * w[2:3, :]

    n_chunks = d_ff // FF_CHUNK
    u_next = up(0)
    down = None
    for c in range(n_chunks):
        u_gate, u_val = u_next
        if c + 1 < n_chunks:
            u_next = up(c + 1)
        gate = conv(c * FF_CHUNK, u_gate)
        val = conv(d_ff + c * FF_CHUNK, u_val)
        act = gate * (1.0 / (1.0 + jnp.exp(-gate))) * val
        part = _dot(act.astype(BF16), wdn_ref[c * FF_CHUNK:(c + 1) * FF_CHUNK, :])
        down = part if down is None else down + part
    acc = h + down
    if final:
        acc = _rms(acc, gfin_ref[...])
    o_ref[...] = acc


def _ffn(h, g, w_up, conv_w, conv_b, w_down, g_final=None):
    S, D = h.shape
    d_ff = w_down.shape[0]
    tm = min(ROW_TILE, S)
    assert d_ff % FF_CHUNK == 0 and conv_w.shape[0] == CONV_WIDTH
    row = lambda i: (i, 0)
    in_specs = [pl.BlockSpec((tm, D), row), _resident((1, D)), _resident(w_up.shape),
                _resident(conv_w.shape), _resident((1, 2 * d_ff)), _resident(w_down.shape)]
    args = [h, g.reshape(1, D), w_up.astype(BF16), conv_w, conv_b.reshape(1, 2 * d_ff), w_down.astype(BF16)]
    if g_final is not None:
        in_specs.append(_resident((1, D)))
        args.append(g_final.reshape(1, D))
    return pl.pallas_call(
        functools.partial(_ffn_kernel, d_ff=d_ff, final=g_final is not None),
        grid=(S // tm,), in_specs=in_specs, out_specs=pl.BlockSpec((tm, D), row),
        out_shape=jax.ShapeDtypeStruct((S, D), F32),
        scratch_shapes=[pltpu.VMEM((8, 2 * d_ff), F32)],
        compiler_params=_params("arbitrary"), name="ffn",
    )(*args)


def kernel(x, mem, positions, norm_mix_g, norm_xa_g, norm_mem_g, norm_ffn_g, ab_w_in, ab_w_out, fox_w_in, fox_b_f, fox_w_out, xa_w_q, xa_w_kv, xa_w_out, ffn_w_up, ffn_conv_w, ffn_conv_b, ffn_w_down, final_norm_g):
    B, S, D = x.shape
    depth = norm_mix_g.shape[0]
    n_pairs = D // LANES
    q_scale2 = HEAD_DIM ** -0.5 * LOG2E
    outs = []
    for b in range(B):
        h = x[b]
        for layer in range(depth):
            n = layer // 2
            if layer % 2 == 0:
                assert D == 2 * PROJ_CHUNK
                half = n_pairs // 2
                w_in = ab_w_in[n]
                qkv, k_moba, vt_moba = _project(
                    h, norm_mix_g[layer], w_in, w_in[:, 2 * D + PROJ_CHUNK:].T,
                    (1.0, q_scale2, 1.0, 1.0, 1.0, 1.0), range(6), (3,), rope=(positions[b], (1, 3)))
                att = [_sb_attention(qkv, 0, n_pairs, 2 * n_pairs, half),
                       _flash_attention("moba", qkv, half, k_moba, vt_moba, half)]
                w_out = ab_w_out[n]
            else:
                w_in = fox_w_in[n]
                q, keys, vt, c2 = _project(
                    h, norm_mix_g[layer], w_in[:, :2 * D], w_in[:, 2 * D:3 * D].T,
                    (q_scale2, q_scale2, 1.0, 1.0), (0, 1), (2, 3), gates=(w_in[:, 3 * D:], fox_b_f[n]))
                kt = min(KEY_TILE["fox"], S)
                c_end = c2[:, kt - 1::kt].reshape(-1)
                att = [_flash_attention("fox", q, 0, keys, vt, n_pairs, c_end)]
                w_out = fox_w_out[n]
            kv = _memory_kv(mem[b], norm_mem_g[layer], xa_w_kv[layer])
            h = _post(att, h, w_out, norm_xa_g[layer], xa_w_q[layer], kv, xa_w_out[layer])
            h = _ffn(h, norm_ffn_g[layer], ffn_w_up[layer], ffn_conv_w[layer], ffn_conv_b[layer],
                     ffn_w_down[layer], final_norm_g if layer == depth - 1 else None)
        outs.append(h)
    return jnp.stack(outs)
```
